```python
import math
import jax, jax.numpy as jnp
from jax import lax
import numpy as np

D_MODEL = 1024
BATCH = 4
SEQ = 8192
DEPTH = 2

N_MIXERS = 2
N_MAMBA = (DEPTH + 1) // 2
N_MOBA = DEPTH // 2

SSM_EXPAND = 2
SSM_D_INNER = SSM_EXPAND * D_MODEL
SSM_HEAD_DIM = 64
SSM_HEADS = SSM_D_INNER // SSM_HEAD_DIM
SSM_GROUPS = 8
SSM_HEADS_PER_GROUP = SSM_HEADS // SSM_GROUPS
SSM_STATE = 128
SSM_CONV = 4
SSM_CHUNK = 256
SSM_CONV_DIM = SSM_D_INNER + 2 * SSM_GROUPS * SSM_STATE
SSM_IN_DIM = SSM_D_INNER + SSM_CONV_DIM + SSM_HEADS

ATTN_HEAD_DIM = 64
ATTN_HEADS = D_MODEL // ATTN_HEAD_DIM
MOBA_BLOCK = 256
MOBA_TOPK = 3
MOBA_Q_CHUNK = 16
ALIBI_MAX_BIAS = 8.0

D_FF = 2816
RMS_EPS = 1e-6

kernel_name = "hybrid_mamba2_moba_macaron_adaln"


def rmsnorm(x, g):
    xf = x.astype(jnp.float32)
    y = xf * lax.rsqrt(jnp.mean(xf * xf, axis=-1, keepdims=True) + RMS_EPS)
    return (y * g.astype(jnp.float32)).astype(x.dtype)


def modulate(x, g, c_act, w, b):
    m = c_act @ w + b
    shift, scale, gate = jnp.split(m, 3, axis=-1)
    h = rmsnorm(x, g) * (1.0 + scale[:, None, :]) + shift[:, None, :]
    return h, gate[:, None, :]


def swiglu(h, w_gate, w_up, w_down):
    return (jax.nn.silu(h @ w_gate) * (h @ w_up)) @ w_down


def ssd_chunked_scan(x, dt, a, bm, cm):
    bsz, L = x.shape[0], x.shape[1]
    n_chunks = -(-L // SSM_CHUNK)
    pad = n_chunks * SSM_CHUNK - L

    def to_chunks(t):
        t = jnp.pad(t.astype(jnp.float32), [(0, 0), (0, pad)] + [(0, 0)] * (t.ndim - 2))
        t = t.reshape((bsz, n_chunks, SSM_CHUNK) + t.shape[2:])
        return jnp.moveaxis(t, 1, 0)

    xc, dtc, bc, cc = to_chunks(x), to_chunks(dt), to_chunks(bm), to_chunks(cm)
    causal = jnp.tril(jnp.ones((SSM_CHUNK, SSM_CHUNK), dtype=bool))[None, :, :, None, None]

    def step(state, inp):
        x_k, dt_k, b_k, c_k = inp
        acum = jnp.cumsum(dt_k * a, axis=1)
        seg = acum[:, :, None] - acum[:, None, :]
        decay = jnp.exp(jnp.where(causal, seg, -jnp.inf))
        cb = jnp.einsum('btgn,bsgn->btsg', c_k, b_k)
        w = cb[..., None] * decay * dt_k[:, None]
        y_intra = jnp.einsum('btsgr,bsgrp->btgrp', w, x_k)
        y_inter = jnp.einsum('btgn,bgrpn->btgrp', c_k, state) * jnp.exp(acum)[..., None]
        to_end = jnp.exp(acum[:, -1:] - acum) * dt_k
        new_state = (state * jnp.exp(acum[:, -1])[..., None, None]
                     + jnp.einsum('bsgn,bsgr,bsgrp->bgrpn', b_k, to_end, x_k))
        return new_state, y_intra + y_inter

    state0 = jnp.zeros((bsz, SSM_GROUPS, SSM_HEADS_PER_GROUP, SSM_HEAD_DIM, SSM_STATE), jnp.float32)
    _, y = lax.scan(step, state0, (xc, dtc, bc, cc))
    y = jnp.moveaxis(y, 0, 1).reshape((bsz, n_chunks * SSM_CHUNK) + y.shape[3:])[:, :L]
    return y.astype(x.dtype)


def mamba2_mixer(h, w_in, conv_w, conv_b, dt_bias, a_log, d_skip, norm_w, w_out):
    bsz, L, _ = h.shape
    G, R, P, N = SSM_GROUPS, SSM_HEADS_PER_GROUP, SSM_HEAD_DIM, SSM_STATE
    proj = h @ w_in
    z, xbc, dt = jnp.split(proj, [SSM_D_INNER, SSM_D_INNER + SSM_CONV_DIM], axis=-1)
    xbc = lax.conv_general_dilated(
        xbc, conv_w, window_strides=(1,), padding=[(SSM_CONV - 1, 0)],
        dimension_numbers=('NWC', 'WIO', 'NWC'), feature_group_count=SSM_CONV_DIM) + conv_b
    xbc = jax.nn.silu(xbc)
    xs, bm, cm = jnp.split(xbc, [SSM_D_INNER, SSM_D_INNER + G * N], axis=-1)
    dt = jax.nn.softplus(dt.astype(jnp.float32) + dt_bias.astype(jnp.float32))
    a = -jnp.exp(a_log.astype(jnp.float32))
    xs = xs.reshape(bsz, L, G, R, P)
    y = ssd_chunked_scan(xs, dt.reshape(bsz, L, G, R), a.reshape(G, R),
                         bm.reshape(bsz, L, G, N), cm.reshape(bsz, L, G, N))
    y = y + d_skip.reshape(G, R)[..., None] * xs
    y = y.reshape(bsz, L, SSM_D_INNER)
    y = rmsnorm(y * jax.nn.silu(z), norm_w)
    return y @ w_out


def moba_mixer(h, w_qkv, w_out):
    bsz, L, _ = h.shape
    H, dh, BLK, QC = ATTN_HEADS, ATTN_HEAD_DIM, MOBA_BLOCK, MOBA_Q_CHUNK
    qkv = (h @ w_qkv).reshape(bsz, L, 3, H, dh)
    q = jnp.transpose(qkv[:, :, 0], (0, 2, 1, 3))
    k = jnp.transpose(qkv[:, :, 1], (0, 2, 1, 3))
    v = jnp.transpose(qkv[:, :, 2], (0, 2, 1, 3))
    n_blocks = -(-L // BLK)
    K = max(1, min(MOBA_TOPK, n_blocks))
    pad = n_blocks * BLK - L
    k = jnp.pad(k, [(0, 0), (0, 0), (0, pad), (0, 0)])
    v = jnp.pad(v, [(0, 0), (0, 0), (0, pad), (0, 0)])
    kb = k.reshape(bsz, H, n_blocks, BLK, dh)
    vb = v.reshape(bsz, H, n_blocks, BLK, dh)
    kmean = jnp.mean(kb.astype(jnp.float32), axis=3)
    scale = dh ** -0.5
    slopes = jnp.exp2(-ALIBI_MAX_BIAS * (jnp.arange(H, dtype=jnp.float32) + 1.0) / H)
    bi = jnp.arange(bsz)[:, None, None, None]
    hi = jnp.arange(H)[None, :, None, None]
    blk_ids = jnp.arange(n_blocks)
    in_blk = jnp.arange(BLK)

    def chunk(ci):
        q0 = ci * QC
        own = q0 // BLK
        qc = lax.dynamic_slice_in_dim(q, q0, QC, axis=2)
        qpos = q0 + jnp.arange(QC)
        gate = jnp.einsum('bhqd,bhnd->bhqn', qc.astype(jnp.float32), kmean)
        gate = jnp.where((blk_ids < own)[None, None, None, :], gate, -jnp.inf)
        _, sel = lax.top_k(gate, K)
        sel_valid = jnp.arange(K) < own
        k_sel = kb[bi, hi, sel]
        v_sel = vb[bi, hi, sel]
        kpos_sel = sel[..., None] * BLK + in_blk
        dist_sel = (qpos[:, None, None] - kpos_sel).astype(jnp.float32)
        s_sel = (jnp.einsum('bhqd,bhqkjd->bhqkj', qc, k_sel).astype(jnp.float32) * scale
                 - slopes[:, None, None, None] * dist_sel)
        s_sel = jnp.where(sel_valid[:, None], s_sel, -jnp.inf)
        k_own = lax.dynamic_slice_in_dim(k, own * BLK, BLK, axis=2)
        v_own = lax.dynamic_slice_in_dim(v, own * BLK, BLK, axis=2)
        kpos_own = own * BLK + in_blk
        dist_own = (qpos[:, None] - kpos_own[None, :]).astype(jnp.float32)
        s_own = (jnp.einsum('bhqd,bhjd->bhqj', qc, k_own).astype(jnp.float32) * scale
                 - slopes[:, None, None] * dist_own)
        s_own = jnp.where(dist_own >= 0.0, s_own, -jnp.inf)
        s = jnp.concatenate([s_own, s_sel.reshape(bsz, H, QC, K * BLK)], axis=-1)
        p = jax.nn.softmax(s, axis=-1).astype(v.dtype)
        p_own = p[..., :BLK]
        p_sel = p[..., BLK:].reshape(bsz, H, QC, K, BLK)
        return (jnp.einsum('bhqj,bhjd->bhqd', p_own, v_own)
                + jnp.einsum('bhqkj,bhqkjd->bhqd', p_sel, v_sel))

    out = lax.map(chunk, jnp.arange(L // QC))
    out = jnp.transpose(out, (1, 0, 3, 2, 4)).reshape(bsz, L, H * dh)
    return out @ w_out


def setup_inputs(seed: int = 0) -> dict:
    key = jax.random.key(seed)
    ks = jax.random.split(key, 24)
    f32 = jnp.float32

    def nrm(k, shape, s):
        return jax.random.normal(k, shape, f32) * s

    x = nrm(ks[0], (BATCH, SEQ, D_MODEL), 1.0)
    c = nrm(ks[1], (BATCH, D_MODEL), 1.0)
    norm_g = 1.0 + nrm(ks[2], (DEPTH, 3, D_MODEL), 0.05)
    mod_w = nrm(ks[3], (DEPTH, 3, D_MODEL, 3 * D_MODEL), 0.5 * D_MODEL ** -0.5)
    mod_b = nrm(ks[4], (DEPTH, 3, 3 * D_MODEL), 0.02)
    ffn_w_gate = nrm(ks[5], (DEPTH, 2, D_MODEL, D_FF), D_MODEL ** -0.5)
    ffn_w_up = nrm(ks[6], (DEPTH, 2, D_MODEL, D_FF), D_MODEL ** -0.5)
    ffn_w_down = nrm(ks[7], (DEPTH, 2, D_FF, D_MODEL), D_FF ** -0.5)
    ssm_w_in = nrm(ks[8], (N_MAMBA, D_MODEL, SSM_IN_DIM), D_MODEL ** -0.5)
    ssm_conv_w = nrm(ks[9], (N_MAMBA, SSM_CONV, 1, SSM_CONV_DIM), SSM_CONV ** -0.5)
    ssm_conv_b = nrm(ks[10], (N_MAMBA, SSM_CONV_DIM), 0.02)
    dt0 = jnp.exp(jax.random.uniform(ks[11], (N_MAMBA, SSM_HEADS), f32,
                                     math.log(1e-3), math.log(1e-1)))
    ssm_dt_bias = dt0 + jnp.log(-jnp.expm1(-dt0))
    ssm_a_log = jnp.log(jax.random.uniform(ks[12], (N_MAMBA, SSM_HEADS), f32, 1.0, 16.0))
    ssm_d = 1.0 + nrm(ks[13], (N_MAMBA, SSM_HEADS), 0.05)
    ssm_norm_w = 1.0 + nrm(ks[14], (N_MAMBA, SSM_D_INNER), 0.05)
    ssm_w_out = nrm(ks[15], (N_MAMBA, SSM_D_INNER, D_MODEL), SSM_D_INNER ** -0.5)
    attn_w_qkv = nrm(ks[16], (N_MOBA, D_MODEL, 3 * D_MODEL), D_MODEL ** -0.5)
    attn_w_out = nrm(ks[17], (N_MOBA, D_MODEL, D_MODEL), D_MODEL ** -0.5)
    final_norm_g = 1.0 + nrm(ks[18], (D_MODEL,), 0.05)
    return {"x": x, "c": c, "norm_g": norm_g, "mod_w": mod_w, "mod_b": mod_b,
            "ffn_w_gate": ffn_w_gate, "ffn_w_up": ffn_w_up, "ffn_w_down": ffn_w_down,
            "ssm_w_in": ssm_w_in, "ssm_conv_w": ssm_conv_w, "ssm_conv_b": ssm_conv_b,
            "ssm_dt_bias": ssm_dt_bias, "ssm_a_log": ssm_a_log, "ssm_d": ssm_d,
            "ssm_norm_w": ssm_norm_w, "ssm_w_out": ssm_w_out,
            "attn_w_qkv": attn_w_qkv, "attn_w_out": attn_w_out,
            "final_norm_g": final_norm_g}


def reference(x, c, norm_g, mod_w, mod_b, ffn_w_gate, ffn_w_up, ffn_w_down,
              ssm_w_in, ssm_conv_w, ssm_conv_b, ssm_dt_bias, ssm_a_log, ssm_d,
              ssm_norm_w, ssm_w_out, attn_w_qkv, attn_w_out, final_norm_g):
    c_act = jax.nn.silu(c)
    for i in range(DEPTH):
        h, g = modulate(x, norm_g[i, 0], c_act, mod_w[i, 0], mod_b[i, 0])
        x = x + 0.5 * g * swiglu(h, ffn_w_gate[i, 0], ffn_w_up[i, 0], ffn_w_down[i, 0])
        h, g = modulate(x, norm_g[i, 1], c_act, mod_w[i, 1], mod_b[i, 1])
        j = i // N_MIXERS
        if i % N_MIXERS == 0:
            y = mamba2_mixer(h, ssm_w_in[j], ssm_conv_w[j], ssm_conv_b[j], ssm_dt_bias[j],
                             ssm_a_log[j], ssm_d[j], ssm_norm_w[j], ssm_w_out[j])
        else:
            y = moba_mixer(h, attn_w_qkv[j], attn_w_out[j])
        x = x + g * y
        h, g = modulate(x, norm_g[i, 2], c_act, mod_w[i, 2], mod_b[i, 2])
        x = x + 0.5 * g * swiglu(h, ffn_w_gate[i, 1], ffn_w_up[i, 1], ffn_w_down[i, 1])
    return rmsnorm(x, final_norm_g)
```

```python
import functools

import jax
import jax.numpy as jnp
from jax import lax
from jax.experimental import pallas as pl
from jax.experimental.pallas import tpu as pltpu

F32 = jnp.float32
BF16 = jnp.bfloat16
HIGHEST = lax.Precision.HIGHEST

RMS_EPS = 1e-6

SSM_HEAD_DIM = 64
SSM_GROUPS = 8
SSM_STATE = 128
SSM_CONV = 4
SSM_CHUNK = 256

ATTN_HEAD_DIM = 64
MOBA_BLOCK = 256
MOBA_TOPK = 3
ALIBI_MAX_BIAS = 8.0

LANES = 128
SUBLANES = 8

VMEM_LIMIT = 56 * 1024 * 1024
NEG_BIG = -(2.0 ** 100)


def _nt(a, b, precision=None):
    return lax.dot_general(a, b, (((1,), (1,)), ((), ())),
                           preferred_element_type=F32, precision=precision)


def _mm(a, b, precision=None):
    return jnp.dot(a, b, preferred_element_type=F32, precision=precision)


def _silu(v):
    return v * jax.nn.sigmoid(v)


def _rms(x, g):
    ms = jnp.mean(x * x, axis=-1, keepdims=True)
    return x * lax.rsqrt(ms + RMS_EPS) * g


def _modulated(x, g, m, d):
    return _rms(x, g) * (1.0 + m[:, d:2 * d]) + m[:, :d]


def _resident(shape):
    return pl.BlockSpec(shape, lambda *_: (0,) * len(shape),
                        pipeline_mode=pl.Buffered(1))


def _params(n_axes):
    return pltpu.CompilerParams(dimension_semantics=("arbitrary",) * n_axes,
                                vmem_limit_bytes=VMEM_LIMIT)


def _mod_kernel(ct_ref, w_ref, b_ref, o_ref):
    ca = _silu(ct_ref[...])
    w = w_ref[...]
    rows = [jnp.sum(w * ca[:, b:b + 1], axis=0, keepdims=True)
            for b in range(ca.shape[1])]
    o_ref[...] = jnp.concatenate(rows, axis=0) + b_ref[...]


def _mod_params(c, mod_w, mod_b):
    bsz, d = c.shape
    n_sets = mod_w.shape[0] * mod_w.shape[1]
    w = mod_w.reshape(n_sets, d, 3 * d)
    b = mod_b.reshape(n_sets, 1, 3 * d)
    tn = 512
    return pl.pallas_call(
        _mod_kernel,
        grid=(n_sets, 3 * d // tn),
        in_specs=[pl.BlockSpec((d, bsz), lambda s, n: (0, 0)),
                  pl.BlockSpec((None, d, tn), lambda s, n: (s, 0, n)),
                  pl.BlockSpec((None, 1, tn), lambda s, n: (s, 0, n))],
        out_specs=pl.BlockSpec((None, bsz, tn), lambda s, n: (s, 0, n)),
        out_shape=jax.ShapeDtypeStruct((n_sets, bsz, 3 * d), F32),
        compiler_params=_params(2),
        name="adaln_params",
    )(c.T, w, b)


def _ffn_kernel(x_ref, m_ref, g_ref, wg_ref, wu_ref, wd_ref, *rest, tf, final):
    o_ref = rest[-1]
    d = x_ref.shape[1]
    x = x_ref[...]
    m = m_ref[...]
    h = _modulated(x, g_ref[...], m, d).astype(BF16)
    acc = None
    for f0 in range(0, wg_ref.shape[1], tf):
        a = _mm(h, wg_ref[:, f0:f0 + tf])
        u = _mm(h, wu_ref[:, f0:f0 + tf])
        act = (_silu(a) * u).astype(BF16)
        y = _mm(act, wd_ref[f0:f0 + tf, :])
        acc = y if acc is None else acc + y
    out = x + (0.5 * m[:, 2 * d:]) * acc
    if final:
        out = _rms(out, rest[0][...])
    o_ref[...] = out


def _ffn(x2, m, g, wg, wu, wd, rows_per_batch, final_g=None):
    t, d = x2.shape
    f = wg.shape[1]
    tm, tf = 512, 256
    per_b = rows_per_batch // tm
    in_specs = [pl.BlockSpec((tm, d), lambda i: (i, 0)),
                pl.BlockSpec((None, 1, 3 * d), lambda i: (i // per_b, 0, 0)),
                _resident((1, d)), _resident((d, f)), _resident((d, f)),
                _resident((f, d))]
    args = [x2, m, g, wg, wu, wd]
    if final_g is not None:
        in_specs.append(_resident((1, d)))
        args.append(final_g)
    return pl.pallas_call(
        functools.partial(_ffn_kernel, tf=tf, final=final_g is not None),
        grid=(t // tm,),
        in_specs=in_specs,
        out_specs=pl.BlockSpec((tm, d), lambda i: (i, 0)),
        out_shape=jax.ShapeDtypeStruct((t, d), F32),
        compiler_params=_params(1),
        name="swiglu_half_step",
    )(*args)


def _ssm_in_kernel(x_ref, m_ref, g_ref, w_ref, wdt_ref, z_ref, xbc_ref, dtt_ref,
                   *, tn):
    d = x_ref.shape[1]
    d_inner = z_ref.shape[1]
    h = _modulated(x_ref[...], g_ref[...], m_ref[...], d).astype(BF16)
    for n0 in range(0, d_inner, tn):
        z_ref[:, n0:n0 + tn] = _mm(h, w_ref[:, n0:n0 + tn]).astype(BF16)
    for n0 in range(0, xbc_ref.shape[1], tn):
        xbc_ref[:, n0:n0 + tn] = _mm(
            h, w_ref[:, d_inner + n0:d_inner + n0 + tn]).astype(BF16)
    dtt_ref[...] = _nt(wdt_ref[...], h)


def _ssm_in(x2, m, g, w_zx, w_dt_t, bsz, seq, d_inner):
    t, d = x2.shape
    n_zx = w_zx.shape[1]
    heads = w_dt_t.shape[0]
    tm = 512
    per_b = seq // tm
    return pl.pallas_call(
        functools.partial(_ssm_in_kernel, tn=512),
        grid=(t // tm,),
        in_specs=[pl.BlockSpec((tm, d), lambda i: (i, 0)),
                  pl.BlockSpec((None, 1, 3 * d), lambda i: (i // per_b, 0, 0)),
                  _resident((1, d)), _resident((d, n_zx)),
                  _resident((heads, d))],
        out_specs=[pl.BlockSpec((tm, d_inner), lambda i: (i, 0)),
                   pl.BlockSpec((tm, n_zx - d_inner), lambda i: (i, 0)),
                   pl.BlockSpec((None, heads, tm),
                                lambda i: (i // per_b, 0, i % per_b))],
        out_shape=[jax.ShapeDtypeStruct((t, d_inner), BF16),
                   jax.ShapeDtypeStruct((t, n_zx - d_inner), BF16),
                   jax.ShapeDtypeStruct((bsz, heads, seq), F32)],
        compiler_params=_params(1),
        name="ssm_in_proj",
    )(x2, m, g, w_zx, w_dt_t)


def _ssd_kernel(xbc_ref, z_ref, dtt_ref, xres_ref, m_ref, cw_ref, cb_ref,
                dtb_ref, alog_ref, dsk_ref, nw_ref, wo_ref, o_ref,
                xpad, xc, xs, y_s, state):
    q = xbc_ref.shape[0]
    conv_dim = xbc_ref.shape[1]
    d_inner = z_ref.shape[1]
    d = xres_ref.shape[1]
    n_groups, gp, n_state = state.shape
    p = SSM_HEAD_DIM
    r_heads = gp // p
    taps = cw_ref.shape[0]
    pad = SUBLANES

    @pl.when(pl.program_id(1) == 0)
    def _():
        state[...] = jnp.zeros_like(state)
        xpad[0:pad, :] = jnp.zeros((pad, conv_dim), F32)

    xpad[pad:pad + q, :] = xbc_ref[...].astype(F32)
    tc = 512
    for c0 in range(0, conv_dim, tc):
        acc = jnp.broadcast_to(cb_ref[:, c0:c0 + tc], (q, tc))
        for k in range(taps):
            r0 = pad - (taps - 1) + k
            acc = acc + cw_ref[k:k + 1, c0:c0 + tc] * xpad[r0:r0 + q, c0:c0 + tc]
        v = _silu(acc)
        xc[:, c0:c0 + tc] = v.astype(BF16)
        if c0 < d_inner:
            xs[:, c0:c0 + tc] = v
    xpad[0:pad, :] = xpad[q:q + pad, :]

    dt = jax.nn.softplus(dtt_ref[...] + dtb_ref[...])
    dta = dt * (-jnp.exp(alog_ref[...]))
    ri = lax.broadcasted_iota(jnp.int32, (q, q), 0)
    ci = lax.broadcasted_iota(jnp.int32, (q, q), 1)
    causal = ri >= ci
    acum_row = _mm(dta, (ri <= ci).astype(F32), HIGHEST)
    acum_col = _nt(causal.astype(F32), dta, HIGHEST)
    to_end = jnp.exp(acum_row[:, q - 1:q] - acum_row) * dt
    exp_acol = jnp.exp(acum_col)
    dec_last = exp_acol[q - 1:q, :]

    b0 = d_inner
    c0_ = d_inner + n_groups * n_state
    for g in range(n_groups):
        bg = xc[:, b0 + g * n_state:b0 + (g + 1) * n_state]
        cg = xc[:, c0_ + g * n_state:c0_ + (g + 1) * n_state]
        cb = _nt(cg, bg)
        st = state[g]
        y_int = _nt(cg, st.astype(BF16))
        ys, te, dec = [], [], []
        for r in range(r_heads):
            h = g * r_heads + r
            seg = acum_col[:, h:h + 1] - acum_row[h:h + 1, :]
            lm = jnp.exp(jnp.where(causal, seg, -jnp.inf))
            w = (cb * lm * dt[h:h + 1, :]).astype(BF16)
            y_h = _mm(w, xc[:, h * p:(h + 1) * p])
            ys.append(y_h + y_int[:, r * p:(r + 1) * p] * exp_acol[:, h:h + 1])
            te.append(jnp.broadcast_to(to_end[h:h + 1, :], (p, q)))
            dec.append(jnp.broadcast_to(dec_last[:, h:h + 1], (p, n_state)))
        xg = xs[:, g * gp:(g + 1) * gp]
        y_s[:, g * gp:(g + 1) * gp] = (jnp.concatenate(ys, axis=1)
                                       + dsk_ref[:, g * gp:(g + 1) * gp] * xg)
        xw = (xg.T * jnp.concatenate(te, axis=0)).astype(BF16)
        state[g] = st * jnp.concatenate(dec, axis=0) + _mm(xw, bg)

    yg = y_s[...] * _silu(z_ref[...].astype(F32))
    yn = _rms(yg, nw_ref[...]).astype(BF16)
    o_ref[...] = xres_ref[...] + m_ref[:, 2 * d:] * _mm(yn, wo_ref[...])


def _ssd(xbc, z, dtt, x2, m, cw, cb, dtb, alog, dsk, nw, wo, bsz, seq):
    t, d = x2.shape
    conv_dim = xbc.shape[1]
    d_inner = z.shape[1]
    heads = dtt.shape[1]
    q = SSM_CHUNK
    nc = seq // q
    gp = d_inner // SSM_GROUPS
    row = lambda b, c: (b * nc + c, 0)
    return pl.pallas_call(
        _ssd_kernel,
        grid=(bsz, nc),
        in_specs=[pl.BlockSpec((q, conv_dim), row),
                  pl.BlockSpec((q, d_inner), row),
                  pl.BlockSpec((None, heads, q), lambda b, c: (b, 0, c)),
                  pl.BlockSpec((q, d), row),
                  pl.BlockSpec((None, 1, 3 * d), lambda b, c: (b, 0, 0)),
                  _resident(cw.shape), _resident(cb.shape),
                  _resident(dtb.shape), _resident(alog.shape),
                  _resident(dsk.shape), _resident(nw.shape),
                  _resident(wo.shape)],
        out_specs=pl.BlockSpec((q, d), row),
        out_shape=jax.ShapeDtypeStruct((t, d), F32),
        scratch_shapes=[pltpu.VMEM((q + 2 * SUBLANES, conv_dim), F32),
                        pltpu.VMEM((q, conv_dim), BF16),
                        pltpu.VMEM((q, d_inner), F32),
                        pltpu.VMEM((q, d_inner), F32),
                        pltpu.VMEM((SSM_GROUPS, gp, SSM_STATE), F32)],
        compiler_params=_params(2),
        name="ssd_chunk_scan",
    )(xbc, z, dtt, x2, m, cw, cb, dtb, alog, dsk, nw, wo)


def _qkv_kernel(x_ref, m_ref, g_ref, w_ref, q_ref, k_ref, v_ref, km_ref, *, tn):
    d = x_ref.shape[1]
    tm = x_ref.shape[0]
    blk = MOBA_BLOCK
    scale = ATTN_HEAD_DIM ** -0.5
    h = _modulated(x_ref[...], g_ref[...], m_ref[...], d).astype(BF16)
    for n0 in range(0, d, tn):
        q_ref[:, n0:n0 + tn] = (_mm(h, w_ref[:, n0:n0 + tn]) * scale).astype(BF16)
        kk = _mm(h, w_ref[:, d + n0:d + n0 + tn])
        k_ref[:, n0:n0 + tn] = kk.astype(BF16)
        for j in range(tm // blk):
            km_ref[j, :, n0:n0 + tn] = jnp.mean(kk[j * blk:(j + 1) * blk, :],
                                                axis=0, keepdims=True)
        v_ref[:, n0:n0 + tn] = _mm(h, w_ref[:, 2 * d + n0:2 * d + n0 + tn]).astype(BF16)


def _qkv(x2, m, g, w, seq):
    t, d = x2.shape
    tm = 512
    per_b = seq // tm
    nb_tile = tm // MOBA_BLOCK
    tok = pl.BlockSpec((tm, d), lambda i: (i, 0))
    return pl.pallas_call(
        functools.partial(_qkv_kernel, tn=512),
        grid=(t // tm,),
        in_specs=[tok,
                  pl.BlockSpec((None, 1, 3 * d), lambda i: (i // per_b, 0, 0)),
                  _resident((1, d)), _resident((d, 3 * d))],
        out_specs=[tok, tok, tok,
                   pl.BlockSpec((nb_tile, 1, d), lambda i: (i, 0, 0))],
        out_shape=[jax.ShapeDtypeStruct((t, d), BF16)] * 3
        + [jax.ShapeDtypeStruct((t // MOBA_BLOCK, 1, d), F32)],
        compiler_params=_params(1),
        name="moba_qkv_proj",
    )(x2, m, g, w)


def _moba_kernel(q_ref, k_ref, v_ref, km_ref, sl_ref, o_ref,
                 sel_s, m_s, l_s, acc_s):
    i = pl.program_id(2)
    blk = q_ref.shape[0]
    n_blocks = km_ref.shape[0]
    dh = ATTN_HEAD_DIM
    q = q_ref[...]
    lane = lax.broadcasted_iota(jnp.int32, (1, LANES), 1)
    ri = lax.broadcasted_iota(jnp.int32, (blk, blk), 0)
    ci = lax.broadcasted_iota(jnp.int32, (blk, blk), 1)
    d0 = (ri - ci).astype(F32)
    eye = (ri == ci).astype(BF16)
    bid = lax.broadcasted_iota(jnp.int32, (n_blocks, blk), 0)
    km = km_ref[...]
    sl = sl_ref[...]
    kd = k_ref[pl.ds(pl.multiple_of(i * blk, blk), blk), :]
    vd = v_ref[pl.ds(pl.multiple_of(i * blk, blk), blk), :]

    qa, slope, in_head = [], [], []
    for j in range(2):
        inj = (lane >= j * dh) & (lane < (j + 1) * dh)
        in_head.append(inj)
        slope_j = jnp.max(jnp.where(inj, sl, 0.0), axis=1, keepdims=True)
        slope.append(slope_j)
        kmj = jnp.where(inj, km, 0.0)
        hi = kmj.astype(BF16)
        lo = (kmj - hi.astype(F32)).astype(BF16)
        gate = _nt(hi, q) + _nt(lo, q)
        gate = jnp.where(bid < i, gate, -jnp.inf)
        bias_t = jnp.full((n_blocks, blk), NEG_BIG, F32)
        for _ in range(MOBA_TOPK):
            mx = jnp.max(gate, axis=0, keepdims=True)
            first = jnp.min(jnp.where(gate == mx, bid, n_blocks), axis=0,
                            keepdims=True)
            pick = (bid == first) & (mx > -jnp.inf)
            bias_t = jnp.where(pick, 0.0, bias_t)
            gate = jnp.where(pick, -jnp.inf, gate)
        off = (1 - j) * dh
        sel_s[...] = jnp.zeros_like(sel_s)
        sel_s[off:off + n_blocks, :] = bias_t.astype(BF16)
        q_extra = _nt(eye, sel_s[...])
        qa.append(jnp.where(inj, q, q_extra.astype(BF16)))

        s = _nt(qa[j], jnp.where(inj, kd, jnp.zeros_like(kd)))
        s = s - slope_j * d0
        s = jnp.where(d0 >= 0.0, s, NEG_BIG)
        mj = jnp.max(s, axis=1, keepdims=True)
        pj = jnp.exp(s - mj)
        m_s[j] = mj
        l_s[j] = jnp.sum(pj, axis=1, keepdims=True)
        acc_s[j] = _mm(pj.astype(BF16), vd)

    def visit(n, carry):
        kn = k_ref[pl.ds(pl.multiple_of(n * blk, blk), blk), :]
        vn = v_ref[pl.ds(pl.multiple_of(n * blk, blk), blk), :]
        dist = d0 + ((i - n) * blk).astype(F32)
        for j in range(2):
            onehot = jnp.where(lane == (1 - j) * dh + n, 1.0, 0.0).astype(BF16)
            ka = jnp.where(in_head[j], kn, jnp.broadcast_to(onehot, kn.shape))
            s = _nt(qa[j], ka) - slope[j] * dist
            m_old = m_s[j]
            m_new = jnp.maximum(m_old, jnp.max(s, axis=1, keepdims=True))
            alpha = jnp.exp(m_old - m_new)
            pj = jnp.exp(s - m_new)
            m_s[j] = m_new
            l_s[j] = alpha * l_s[j] + jnp.sum(pj, axis=1, keepdims=True)
            acc_s[j] = alpha * acc_s[j] + _mm(pj.astype(BF16), vn)
        return carry

    lax.fori_loop(0, i, visit, 0)
    o_ref[...] = jnp.where(in_head[0], acc_s[0] / l_s[0],
                           acc_s[1] / l_s[1]).astype(o_ref.dtype)


def _moba_attention(q, k, v, kmean, slopes, bsz, seq):
    t, d = q.shape
    blk = MOBA_BLOCK
    nb = seq // blk
    n_pairs = d // LANES
    k3 = k.reshape(bsz, seq, d)
    v3 = v.reshape(bsz, seq, d)
    qspec = pl.BlockSpec((blk, LANES), lambda b, hp, i: (b * nb + i, hp))
    kvspec = pl.BlockSpec((None, seq, LANES), lambda b, hp, i: (b, 0, hp))
    return pl.pallas_call(
        _moba_kernel,
        grid=(bsz, n_pairs, nb),
        in_specs=[qspec, kvspec, kvspec,
                  pl.BlockSpec((None, nb, LANES), lambda b, hp, i: (b, 0, hp)),
                  pl.BlockSpec((None, 1, LANES), lambda b, hp, i: (hp, 0, 0))],
        out_specs=qspec,
        out_shape=jax.ShapeDtypeStruct((t, d), BF16),
        scratch_shapes=[pltpu.VMEM((LANES, blk), BF16),
                        pltpu.VMEM((2, blk, 1), F32),
                        pltpu.VMEM((2, blk, 1), F32),
                        pltpu.VMEM((2, blk, LANES), F32)],
        compiler_params=_params(3),
        name="moba_attention",
    )(q, k3, v3, kmean, slopes)


def _proj_res_kernel(a_ref, x_ref, m_ref, w_ref, o_ref):
    d = x_ref.shape[1]
    o_ref[...] = x_ref[...] + m_ref[:, 2 * d:] * _mm(a_ref[...], w_ref[...])


def _proj_res(a, x2, m, w, rows_per_batch):
    t, d = x2.shape
    tm = 512
    per_b = rows_per_batch // tm
    return pl.pallas_call(
        _proj_res_kernel,
        grid=(t // tm,),
        in_specs=[pl.BlockSpec((tm, a.shape[1]), lambda i: (i, 0)),
                  pl.BlockSpec((tm, d), lambda i: (i, 0)),
                  pl.BlockSpec((None, 1, 3 * d), lambda i: (i // per_b, 0, 0)),
                  _resident(w.shape)],
        out_specs=pl.BlockSpec((tm, d), lambda i: (i, 0)),
        out_shape=jax.ShapeDtypeStruct((t, d), F32),
        compiler_params=_params(1),
        name="attn_out_proj",
    )(a, x2, m, w)


def kernel(x, c, norm_g, mod_w, mod_b, ffn_w_gate, ffn_w_up, ffn_w_down,
           ssm_w_in, ssm_conv_w, ssm_conv_b, ssm_dt_bias, ssm_a_log, ssm_d,
           ssm_norm_w, ssm_w_out, attn_w_qkv, attn_w_out, final_norm_g):
    bsz, seq, d = x.shape
    depth = norm_g.shape[0]
    d_inner = ssm_w_out.shape[1]
    heads = ssm_dt_bias.shape[1]
    n_attn_heads = d // ATTN_HEAD_DIM

    m_all = _mod_params(c, mod_w, mod_b)
    x2 = x.reshape(bsz * seq, d)
    slopes = jnp.exp2(-ALIBI_MAX_BIAS
                      * (jnp.arange(n_attn_heads, dtype=F32) + 1.0) / n_attn_heads)
    slopes = jnp.repeat(slopes, ATTN_HEAD_DIM).reshape(d // LANES, 1, LANES)

    for i in range(depth):
        ms = [m_all[3 * i + j].reshape(bsz, 1, 3 * d) for j in range(3)]
        gs = [norm_g[i, j].reshape(1, d) for j in range(3)]
        x2 = _ffn(x2, ms[0], gs[0], ffn_w_gate[i, 0].astype(BF16),
                  ffn_w_up[i, 0].astype(BF16), ffn_w_down[i, 0].astype(BF16), seq)
        j = i // 2
        if i % 2 == 0:
            n_zx = ssm_w_in.shape[2] - heads
            z, xbc, dtt = _ssm_in(x2, ms[1], gs[1],
                                  ssm_w_in[j][:, :n_zx].astype(BF16),
                                  ssm_w_in[j][:, n_zx:].T.astype(BF16),
                                  bsz, seq, d_inner)
            x2 = _ssd(xbc, z, dtt, x2, ms[1],
                      ssm_conv_w[j].reshape(SSM_CONV, -1),
                      ssm_conv_b[j].reshape(1, -1),
                      ssm_dt_bias[j].reshape(heads, 1),
                      ssm_a_log[j].reshape(heads, 1),
                      jnp.repeat(ssm_d[j], SSM_HEAD_DIM).reshape(1, d_inner),
                      ssm_norm_w[j].reshape(1, d_inner),
                      ssm_w_out[j].astype(BF16), bsz, seq)
        else:
            q, k, v, km = _qkv(x2, ms[1], gs[1], attn_w_qkv[j].astype(BF16), seq)
            att = _moba_attention(q, k, v, km.reshape(bsz, seq // MOBA_BLOCK, d),
                                  slopes, bsz, seq)
            x2 = _proj_res(att, x2, ms[1], attn_w_out[j].astype(BF16), seq)
        last = i == depth - 1
        x2 = _ffn(x2, ms[2], gs[2], ffn_w_gate[i, 1].astype(BF16),
                  ffn_w_up[i, 1].astype(BF16), ffn_w_down[i, 1].astype(BF16), seq,
                  final_g=final_norm_g.reshape(1, d) if last else None)
    return x2.reshape(bsz, seq, d)
```

```python
import functools

import jax
import jax.numpy as jnp
from jax import lax
from jax.experimental import pallas as pl
from jax.experimental.pallas import tpu as pltpu

F32 = jnp.float32
BF16 = jnp.bfloat16
HIGHEST = lax.Precision.HIGHEST

RMS_EPS = 1e-6

SSM_HEAD_DIM = 64
SSM_GROUPS = 8
SSM_STATE = 128
SSM_CONV = 4
SSM_CHUNK = 256

ATTN_HEAD_DIM = 64
MOBA_BLOCK = 256
MOBA_TOPK = 3
ALIBI_MAX_BIAS = 8.0

LANES = 128
SUBLANES = 8

VMEM_LIMIT = 56 * 1024 * 1024
NEG_BIG = -(2.0 ** 100)


def _nt(a, b, precision=None):
    return lax.dot_general(a, b, (((1,), (1,)), ((), ())),
                           preferred_element_type=F32, precision=precision)


def _mm(a, b, precision=None):
    return jnp.dot(a, b, preferred_element_type=F32, precision=precision)


def _silu(v):
    return v * jax.nn.sigmoid(v)


def _rms(x, g):
    ms = jnp.mean(x * x, axis=-1, keepdims=True)
    return x * lax.rsqrt(ms + RMS_EPS) * g


def _modulated(x, g, m, d):
    return _rms(x, g) * (1.0 + m[:, d:2 * d]) + m[:, :d]


def _resident(shape):
    return pl.BlockSpec(shape, lambda *_: (0,) * len(shape),
                        pipeline_mode=pl.Buffered(1))


def _params(n_axes):
    return pltpu.CompilerParams(dimension_semantics=("arbitrary",) * n_axes,
                                vmem_limit_bytes=VMEM_LIMIT)


def _mod_kernel(ct_ref, w_ref, b_ref, o_ref):
    ca = _silu(ct_ref[...])
    w = w_ref[...]
    rows = [jnp.sum(w * ca[:, b:b + 1], axis=0, keepdims=True)
            for b in range(ca.shape[1])]
    o_ref[...] = jnp.concatenate(rows, axis=0) + b_ref[...]


def _mod_params(c, mod_w, mod_b):
    bsz, d = c.shape
    n_sets = mod_w.shape[0] * mod_w.shape[1]
    w = mod_w.reshape(n_sets, d, 3 * d)
    b = mod_b.reshape(n_sets, 1, 3 * d)
    tn = 512
    return pl.pallas_call(
        _mod_kernel,
        grid=(n_sets, 3 * d // tn),
        in_specs=[pl.BlockSpec((d, bsz), lambda s, n: (0, 0)),
                  pl.BlockSpec((None, d, tn), lambda s, n: (s, 0, n)),
                  pl.BlockSpec((None, 1, tn), lambda s, n: (s, 0, n))],
        out_specs=pl.BlockSpec((None, bsz, tn), lambda s, n: (s, 0, n)),
        out_shape=jax.ShapeDtypeStruct((n_sets, bsz, 3 * d), F32),
        compiler_params=_params(2),
        name="adaln_params",
    )(c.T, w, b)


def _ffn_kernel(x_ref, m_ref, g_ref, wg_ref, wu_ref, wd_ref, *rest, tf, final):
    o_ref = rest[-1]
    d = x_ref.shape[1]
    x = x_ref[...]
    m = m_ref[...]
    h = _modulated(x, g_ref[...], m, d).astype(BF16)
    acc = None
    for f0 in range(0, wg_ref.shape[1], tf):
        a = _mm(h, wg_ref[:, f0:f0 + tf])
        u = _mm(h, wu_ref[:, f0:f0 + tf])
        act = (_silu(a) * u).astype(BF16)
        y = _mm(act, wd_ref[f0:f0 + tf, :])
        acc = y if acc is None else acc + y
    out = x + (0.5 * m[:, 2 * d:]) * acc
    if final:
        out = _rms(out, rest[0][...])
    o_ref[...] = out


def _ffn(x2, m, g, wg, wu, wd, rows_per_batch, final_g=None):
    t, d = x2.shape
    f = wg.shape[1]
    tm, tf = 512, 256
    per_b = rows_per_batch // tm
    in_specs = [pl.BlockSpec((tm, d), lambda i: (i, 0)),
                pl.BlockSpec((None, 1, 3 * d), lambda i: (i // per_b, 0, 0)),
                _resident((1, d)), _resident((d, f)), _resident((d, f)),
                _resident((f, d))]
    args = [x2, m, g, wg, wu, wd]
    if final_g is not None:
        in_specs.append(_resident((1, d)))
        args.append(final_g)
    return pl.pallas_call(
        functools.partial(_ffn_kernel, tf=tf, final=final_g is not None),
        grid=(t // tm,),
        in_specs=in_specs,
        out_specs=pl.BlockSpec((tm, d), lambda i: (i, 0)),
        out_shape=jax.ShapeDtypeStruct((t, d), F32),
        compiler_params=_params(1),
        name="swiglu_half_step",
    )(*args)


def _ssm_in_kernel(x_ref, m_ref, g_ref, w_ref, wdt_ref, z_ref, xbc_ref, dtt_ref,
                   *, tn):
    d = x_ref.shape[1]
    d_inner = z_ref.shape[1]
    h = _modulated(x_ref[...], g_ref[...], m_ref[...], d).astype(BF16)
    for n0 in range(0, d_inner, tn):
        z_ref[:, n0:n0 + tn] = _mm(h, w_ref[:, n0:n0 + tn]).astype(BF16)
    for n0 in range(0, xbc_ref.shape[1], tn):
        xbc_ref[:, n0:n0 + tn] = _mm(
            h, w_ref[:, d_inner + n0:d_inner + n0 + tn]).astype(BF16)
    dtt_ref[...] = _nt(wdt_ref[...], h)


def _ssm_in(x2, m, g, w_zx, w_dt_t, bsz, seq, d_inner):
    t, d = x2.shape
    n_zx = w_zx.shape[1]
    heads = w_dt_t.shape[0]
    tm = 512
    per_b = seq // tm
    return pl.pallas_call(
        functools.partial(_ssm_in_kernel, tn=512),
        grid=(t // tm,),
        in_specs=[pl.BlockSpec((tm, d), lambda i: (i, 0)),
                  pl.BlockSpec((None, 1, 3 * d), lambda i: (i // per_b, 0, 0)),
                  _resident((1, d)), _resident((d, n_zx)),
                  _resident((heads, d))],
        out_specs=[pl.BlockSpec((tm, d_inner), lambda i: (i, 0)),
                   pl.BlockSpec((tm, n_zx - d_inner), lambda i: (i, 0)),
                   pl.BlockSpec((None, heads, tm),
                                lambda i: (i // per_b, 0, i % per_b))],
        out_shape=[jax.ShapeDtypeStruct((t, d_inner), BF16),
                   jax.ShapeDtypeStruct((t, n_zx - d_inner), BF16),
                   jax.ShapeDtypeStruct((bsz, heads, seq), F32)],
        compiler_params=_params(1),
        name="ssm_in_proj",
    )(x2, m, g, w_zx, w_dt_t)


def _ssd_kernel(xbc_ref, z_ref, dtt_ref, xres_ref, m_ref, cw_ref, cb_ref,
                dtb_ref, alog_ref, dsk_ref, nw_ref, wo_ref, o_ref,
                xpad, xc, xs, y_s, state):
    q = xbc_ref.shape[0]
    conv_dim = xbc_ref.shape[1]
    d_inner = z_ref.shape[1]
    d = xres_ref.shape[1]
    n_groups, gp, n_state = state.shape
    p = SSM_HEAD_DIM
    r_heads = gp // p
    taps = cw_ref.shape[0]
    pad = SUBLANES

    @pl.when(pl.program_id(1) == 0)
    def _():
        state[...] = jnp.zeros_like(state)
        xpad[0:pad, :] = jnp.zeros((pad, conv_dim), F32)

    xpad[pad:pad + q, :] = xbc_ref[...].astype(F32)
    tc = 512
    for c0 in range(0, conv_dim, tc):
        acc = jnp.broadcast_to(cb_ref[:, c0:c0 + tc], (q, tc))
        for k in range(taps):
            r0 = pad - (taps - 1) + k
            acc = acc + cw_ref[k:k + 1, c0:c0 + tc] * xpad[r0:r0 + q, c0:c0 + tc]
        v = _silu(acc)
        xc[:, c0:c0 + tc] = v.astype(BF16)
        if c0 < d_inner:
            xs[:, c0:c0 + tc] = v
    xpad[0:pad, :] = xpad[q:q + pad, :]

    dt = jax.nn.softplus(dtt_ref[...] + dtb_ref[...])
    dta = dt * (-jnp.exp(alog_ref[...]))
    ri = lax.broadcasted_iota(jnp.int32, (q, q), 0)
    ci = lax.broadcasted_iota(jnp.int32, (q, q), 1)
    causal = ri >= ci
    acum_row = _mm(dta, (ri <= ci).astype(F32), HIGHEST)
    acum_col = _nt(causal.astype(F32), dta, HIGHEST)
    to_end = jnp.exp(acum_row[:, q - 1:q] - acum_row) * dt
    exp_acol = jnp.exp(acum_col)
    dec_last = exp_acol[q - 1:q, :]

    b0 = d_inner
    c0_ = d_inner + n_groups * n_state
    for g in range(n_groups):
        bg = xc[:, b0 + g * n_state:b0 + (g + 1) * n_state]
        cg = xc[:, c0_ + g * n_state:c0_ + (g + 1) * n_state]
        cb = _nt(cg, bg)
        st = state[g]
        y_int = _nt(cg, st.astype(BF16))
        ys, te, dec = [], [], []
        for r in range(r_heads):
            h = g * r_heads + r
            seg = acum_col[:, h:h + 1] - acum_row[h:h + 1, :]
            lm = jnp.exp(jnp.where(causal, seg, -jnp.inf))
            w = (cb * lm * dt[h:h + 1, :]).astype(BF16)
            y_h = _mm(w, xc[:, h * p:(h + 1) * p])
            ys.append(y_h + y_int[:, r * p:(r + 1) * p] * exp_acol[:, h:h + 1])
            te.append(jnp.broadcast_to(to_end[h:h + 1, :], (p, q)))
            dec.append(jnp.broadcast_to(dec_last[:, h:h + 1], (p, n_state)))
        xg = xs[:, g * gp:(g + 1) * gp]
        y_s[:, g * gp:(g + 1) * gp] = (jnp.concatenate(ys, axis=1)
                                       + dsk_ref[:, g * gp:(g + 1) * gp] * xg)
        xw = (xg.T * jnp.concatenate(te, axis=0)).astype(BF16)
        state[g] = st * jnp.concatenate(dec, axis=0) + _mm(xw, bg)

    yg = y_s[...] * _silu(z_ref[...].astype(F32))
    yn = _rms(yg, nw_ref[...]).astype(BF16)
    o_ref[...] = xres_ref[...] + m_ref[:, 2 * d:] * _mm(yn, wo_ref[...])


def _ssd(xbc, z, dtt, x2, m, cw, cb, dtb, alog, dsk, nw, wo, bsz, seq):
    t, d = x2.shape
    conv_dim = xbc.shape[1]
    d_inner = z.shape[1]
    heads = dtt.shape[1]
    q = SSM_CHUNK
    nc = seq // q
    gp = d_inner // SSM_GROUPS
    row = lambda b, c: (b * nc + c, 0)
    return pl.pallas_call(
        _ssd_kernel,
        grid=(bsz, nc),
        in_specs=[pl.BlockSpec((q, conv_dim), row),
                  pl.BlockSpec((q, d_inner), row),
                  pl.BlockSpec((None, heads, q), lambda b, c: (b, 0, c)),
                  pl.BlockSpec((q, d), row),
                  pl.BlockSpec((None, 1, 3 * d), lambda b, c: (b, 0, 0)),
                  _resident(cw.shape), _resident(cb.shape),
                  _resident(dtb.shape), _resident(alog.shape),
                  _resident(dsk.shape), _resident(nw.shape),
                  _resident(wo.shape)],
        out_specs=pl.BlockSpec((q, d), row),
        out_shape=jax.ShapeDtypeStruct((t, d), F32),
        scratch_shapes=[pltpu.VMEM((q + 2 * SUBLANES, conv_dim), F32),
                        pltpu.VMEM((q, conv_dim), BF16),
                        pltpu.VMEM((q, d_inner), F32),
                        pltpu.VMEM((q, d_inner), F32),
                        pltpu.VMEM((SSM_GROUPS, gp, SSM_STATE), F32)],
        compiler_params=_params(2),
        name="ssd_chunk_scan",
    )(xbc, z, dtt, x2, m, cw, cb, dtb, alog, dsk, nw, wo)


KPOS_LANE = 96
SLOT = LANES


def _qkv_kernel(x_ref, m_ref, g_ref, wqt_ref, wk_ref, wvt_ref,
                qt_ref, k_ref, vt_ref, km_ref, *, per_b):
    tm, d = x_ref.shape
    blk = MOBA_BLOCK
    dh = ATTN_HEAD_DIM
    nbt = tm // blk
    scale = dh ** -0.5
    h = _modulated(x_ref[...], g_ref[...], m_ref[...], d).astype(BF16)

    tr = 256
    for r0 in range(0, d, tr):
        qt_ref[r0:r0 + tr, :] = (_nt(wqt_ref[r0:r0 + tr, :], h) * scale).astype(BF16)

    tn = 512
    row = lax.broadcasted_iota(jnp.int32, (tm, tn), 0)
    lane = lax.broadcasted_iota(jnp.int32, (tm, tn), 1) % SLOT
    blk_id = (pl.program_id(0) % per_b) * nbt + row // blk
    pos = (row % blk).astype(F32)
    extras = jnp.where(lane == dh + blk_id, 1.0,
                       jnp.where((lane >= KPOS_LANE) & (lane < KPOS_LANE + 3),
                                 pos, 0.0))
    for n0 in range(0, k_ref.shape[1], tn):
        kk = _mm(h, wk_ref[:, n0:n0 + tn])
        k_ref[:, n0:n0 + tn] = (kk + extras).astype(BF16)
        for j in range(nbt):
            km_ref[j, :, n0:n0 + tn] = jnp.mean(kk[j * blk:(j + 1) * blk, :],
                                                axis=0, keepdims=True)

    ones_row = lax.broadcasted_iota(jnp.int32, (tr, tm), 0) % SLOT == dh
    for r0 in range(0, vt_ref.shape[1], tr):
        vt = jnp.where(ones_row, 1.0, _nt(wvt_ref[r0:r0 + tr, :], h)).astype(BF16)
        for j in range(nbt):
            vt_ref[j, r0:r0 + tr, :] = vt[:, j * blk:(j + 1) * blk]


def _qkv(x2, m, g, wq_t, wk_pad, wv_t_pad, bsz, seq):
    t, d = x2.shape
    dp = wk_pad.shape[1]
    blk = MOBA_BLOCK
    tm = 512
    per_b = seq // tm
    nbt = tm // blk
    return pl.pallas_call(
        functools.partial(_qkv_kernel, per_b=per_b),
        grid=(t // tm,),
        in_specs=[pl.BlockSpec((tm, d), lambda i: (i, 0)),
                  pl.BlockSpec((None, 1, 3 * d), lambda i: (i // per_b, 0, 0)),
                  _resident((1, d)), _resident(wq_t.shape),
                  _resident(wk_pad.shape), _resident(wv_t_pad.shape)],
        out_specs=[pl.BlockSpec((None, d, tm), lambda i: (i // per_b, 0, i % per_b)),
                   pl.BlockSpec((tm, dp), lambda i: (i, 0)),
                   pl.BlockSpec((None, nbt, dp, blk),
                                lambda i: (i // per_b, i % per_b, 0, 0)),
                   pl.BlockSpec((nbt, 1, dp), lambda i: (i, 0, 0))],
        out_shape=[jax.ShapeDtypeStruct((bsz, d, seq), BF16),
                   jax.ShapeDtypeStruct((t, dp), BF16),
                   jax.ShapeDtypeStruct((bsz, seq // blk, dp, blk), BF16),
                   jax.ShapeDtypeStruct((t // blk, 1, dp), F32)],
        compiler_params=_params(1),
        name="moba_qkv_proj",
    )(x2, m, g, wq_t, wk_pad, wv_t_pad)


MOBA_HEADS_PER_STEP = 4


def _moba_kernel(sl_ref, qt_ref, k_ref, vt_ref, km_ref, aug_ref, o_ref, s_buf):
    hpg = aug_ref.shape[0]
    i = pl.program_id(2)
    blk = qt_ref.shape[1]
    dh = qt_ref.shape[0] // hpg
    nb = km_ref.shape[0]
    heads = range(hpg)
    step = [sl_ref[pl.program_id(1) * hpg + hh] * blk for hh in heads]

    bid = lax.broadcasted_iota(jnp.int32, (nb, blk), 0)
    qa = []
    for hh in heads:
        qt = qt_ref[hh * dh:(hh + 1) * dh, :]
        km = km_ref[:, hh * SLOT:hh * SLOT + dh]
        hi = km.astype(BF16)
        lo = (km - hi.astype(F32)).astype(BF16)
        gate = _mm(hi, qt) + _mm(lo, qt)
        gate = jnp.where(bid < i, gate, -jnp.inf)
        bias = jnp.where(bid == i, 0.0, NEG_BIG)
        for _ in range(MOBA_TOPK):
            mx = jnp.max(gate, axis=0, keepdims=True)
            first = jnp.min(jnp.where(gate == mx, bid, nb), axis=0, keepdims=True)
            pick = (bid == first) & (mx > -jnp.inf)
            bias = jnp.where(pick, 0.0, bias)
            gate = jnp.where(pick, -jnp.inf, gate)
        qa.append(jnp.concatenate([qt, bias.astype(BF16), aug_ref[hh]], axis=0))

    kid = lax.broadcasted_iota(jnp.int32, (blk, blk), 0)
    qid = lax.broadcasted_iota(jnp.int32, (blk, blk), 1)
    causal = kid <= qid
    own0 = pl.multiple_of(i * blk, blk)

    def k_rows(hh, r0, rows):
        return k_ref[pl.ds(r0, rows), hh * SLOT:(hh + 1) * SLOT]

    def v_blk(hh, n):
        return vt_ref[n, hh * SLOT:(hh + 1) * SLOT, :]

    n_pairs = (i + 1) // 2
    last_pair = jnp.maximum(n_pairs - 1, 0)

    def stage_scores(buf, t):
        r0 = pl.multiple_of(jnp.minimum(t, last_pair) * (2 * blk), 2 * blk)
        tops = []
        for hh in heads:
            s = _mm(k_rows(hh, r0, 2 * blk), qa[hh])
            s_buf[buf, hh] = s
            tops.append((jnp.max(s[:blk], axis=0, keepdims=True),
                         jnp.max(s[blk:], axis=0, keepdims=True)))
        return tuple(tops)

    def absorb(buf, t, tops, state):
        n0 = 2 * t
        n1 = n0 + 1
        out = []
        for hh in heads:
            m, acc = state[hh]
            c0 = jnp.where(n0 < i, step[hh] * (n0 - i).astype(F32), NEG_BIG)
            c1 = jnp.where(n1 < i, step[hh] * (n1 - i).astype(F32), NEG_BIG)
            m_new = jnp.maximum(m, jnp.maximum(tops[hh][0] + c0, tops[hh][1] + c1))
            p0 = jnp.exp(s_buf[buf, hh, :blk] - (m_new - c0)).astype(BF16)
            p1 = jnp.exp(s_buf[buf, hh, blk:] - (m_new - c1)).astype(BF16)
            acc = (acc * jnp.exp(m - m_new)
                   + _mm(v_blk(hh, jnp.minimum(n0, nb - 1)), p0)
                   + _mm(v_blk(hh, jnp.minimum(n1, nb - 1)), p1))
            out.append((m_new, acc))
        return tuple(out)

    s_own = [_mm(k_rows(hh, own0, blk), qa[hh]) for hh in heads]
    tops_a = stage_scores(0, 0)
    state = []
    for hh in heads:
        s = jnp.where(causal, s_own[hh], NEG_BIG)
        m = jnp.max(s, axis=0, keepdims=True)
        state.append((m, _mm(v_blk(hh, i), jnp.exp(s - m).astype(BF16))))

    def visit_two_pairs(t2, carry):
        state, tops_a = carry
        t = 2 * t2
        tops_b = stage_scores(1, t + 1)
        state = absorb(0, t, tops_a, state)
        tops_a = stage_scores(0, t + 2)
        state = absorb(1, t + 1, tops_b, state)
        return state, tops_a

    fin, _ = lax.fori_loop(0, (n_pairs + 1) // 2, visit_two_pairs,
                           (tuple(state), tops_a))
    for hh in heads:
        acc = fin[hh][1]
        o_ref[hh * dh:(hh + 1) * dh, :] = (acc[:dh] / acc[dh:dh + 1]).astype(o_ref.dtype)


def _moba_attention(qt, kpad, vt, kmean, aug, slopes, bsz, seq):
    d = qt.shape[1]
    dh = ATTN_HEAD_DIM
    blk = MOBA_BLOCK
    nb = seq // blk
    hpg = MOBA_HEADS_PER_STEP
    n_groups = d // dh // hpg
    dp = kpad.shape[1]
    qspec = pl.BlockSpec((None, hpg * dh, blk), lambda b, h, i: (b, h, i))
    return pl.pallas_call(
        _moba_kernel,
        grid=(bsz, n_groups, nb),
        in_specs=[pl.BlockSpec(memory_space=pltpu.SMEM),
                  qspec,
                  pl.BlockSpec((None, seq, hpg * SLOT), lambda b, h, i: (b, 0, h)),
                  pl.BlockSpec((None, nb, hpg * SLOT, blk), lambda b, h, i: (b, 0, h, 0)),
                  pl.BlockSpec((None, nb, hpg * SLOT), lambda b, h, i: (b, 0, h)),
                  pl.BlockSpec((hpg, SLOT - dh - nb, blk), lambda b, h, i: (h, 0, 0))],
        out_specs=qspec,
        out_shape=jax.ShapeDtypeStruct((bsz, d, seq), BF16),
        scratch_shapes=[pltpu.VMEM((2, hpg, 2 * blk, blk), F32)],
        compiler_params=_params(3),
        name="moba_attention",
    )(slopes, qt, kpad.reshape(bsz, seq, dp), vt, kmean, aug)


def _proj_res_kernel(at_ref, x_ref, m_ref, w_ref, o_ref):
    d = x_ref.shape[1]
    y = lax.dot_general(at_ref[...], w_ref[...], (((0,), (0,)), ((), ())),
                        preferred_element_type=F32)
    o_ref[...] = x_ref[...] + m_ref[:, 2 * d:] * y


def _proj_res(at, x2, m, w, seq):
    t, d = x2.shape
    tm = 512
    per_b = seq // tm
    return pl.pallas_call(
        _proj_res_kernel,
        grid=(t // tm,),
        in_specs=[pl.BlockSpec((None, at.shape[1], tm),
                               lambda i: (i // per_b, 0, i % per_b)),
                  pl.BlockSpec((tm, d), lambda i: (i, 0)),
                  pl.BlockSpec((None, 1, 3 * d), lambda i: (i // per_b, 0, 0)),
                  _resident(w.shape)],
        out_specs=pl.BlockSpec((tm, d), lambda i: (i, 0)),
        out_shape=jax.ShapeDtypeStruct((t, d), F32),
        compiler_params=_params(1),
        name="attn_out_proj",
    )(at, x2, m, w)


def _moba_constants(d, seq):
    n_heads = d // ATTN_HEAD_DIM
    nb = seq // MOBA_BLOCK
    slopes = jnp.exp2(-ALIBI_MAX_BIAS * (jnp.arange(n_heads, dtype=F32) + 1.0) / n_heads)
    hi = slopes.astype(BF16)
    lo = (slopes - hi.astype(F32)).astype(BF16)
    lo2 = (slopes - hi.astype(F32) - lo.astype(F32)).astype(BF16)
    rows = SLOT - ATTN_HEAD_DIM - nb
    aug = jnp.zeros((n_heads, rows), BF16)
    first = KPOS_LANE - ATTN_HEAD_DIM - nb
    aug = aug.at[:, first].set(hi).at[:, first + 1].set(lo).at[:, first + 2].set(lo2)
    return slopes, jnp.broadcast_to(aug[:, :, None], (n_heads, rows, MOBA_BLOCK))


def _pad_heads(w, axis):
    shape = list(w.shape)
    n_heads = shape[axis] // ATTN_HEAD_DIM
    w = w.reshape(shape[:axis] + [n_heads, ATTN_HEAD_DIM] + shape[axis + 1:])
    pads = [(0, 0)] * w.ndim
    pads[axis + 1] = (0, SLOT - ATTN_HEAD_DIM)
    w = jnp.pad(w, pads)
    return w.reshape(shape[:axis] + [n_heads * SLOT] + shape[axis + 1:])


def kernel(x, c, norm_g, mod_w, mod_b, ffn_w_gate, ffn_w_up, ffn_w_down,
           ssm_w_in, ssm_conv_w, ssm_conv_b, ssm_dt_bias, ssm_a_log, ssm_d,
           ssm_norm_w, ssm_w_out, attn_w_qkv, attn_w_out, final_norm_g):
    bsz, seq, d = x.shape
    depth = norm_g.shape[0]
    d_inner = ssm_w_out.shape[1]
    heads = ssm_dt_bias.shape[1]
    assert seq // MOBA_BLOCK == SLOT - ATTN_HEAD_DIM - (SLOT - KPOS_LANE)

    m_all = _mod_params(c, mod_w, mod_b)
    x2 = x.reshape(bsz * seq, d)
    slopes, aug = _moba_constants(d, seq)

    for i in range(depth):
        ms = [m_all[3 * i + j].reshape(bsz, 1, 3 * d) for j in range(3)]
        gs = [norm_g[i, j].reshape(1, d) for j in range(3)]
        x2 = _ffn(x2, ms[0], gs[0], ffn_w_gate[i, 0].astype(BF16),
                  ffn_w_up[i, 0].astype(BF16), ffn_w_down[i, 0].astype(BF16), seq)
        j = i // 2
        if i % 2 == 0:
            n_zx = ssm_w_in.shape[2] - heads
            z, xbc, dtt = _ssm_in(x2, ms[1], gs[1],
                                  ssm_w_in[j][:, :n_zx].astype(BF16),
                                  ssm_w_in[j][:, n_zx:].T.astype(BF16),
                                  bsz, seq, d_inner)
            x2 = _ssd(xbc, z, dtt, x2, ms[1],
                      ssm_conv_w[j].reshape(SSM_CONV, -1),
                      ssm_conv_b[j].reshape(1, -1),
                      ssm_dt_bias[j].reshape(heads, 1),
                      ssm_a_log[j].reshape(heads, 1),
                      jnp.repeat(ssm_d[j], SSM_HEAD_DIM).reshape(1, d_inner),
                      ssm_norm_w[j].reshape(1, d_inner),
                      ssm_w_out[j].astype(BF16), bsz, seq)
        else:
            wqkv = attn_w_qkv[j].astype(BF16)
            qt, kpad, vt, km = _qkv(x2, ms[1], gs[1], wqkv[:, :d].T,
                                    _pad_heads(wqkv[:, d:2 * d], 1),
                                    _pad_heads(wqkv[:, 2 * d:].T, 0), bsz, seq)
            att_t = _moba_attention(qt, kpad, vt,
                                    km.reshape(bsz, seq // MOBA_BLOCK, -1),
                                    aug, slopes, bsz, seq)
            x2 = _proj_res(att_t, x2, ms[1], attn_w_out[j].astype(BF16), seq)
        last = i == depth - 1
        x2 = _ffn(x2, ms[2], gs[2], ffn_w_gate[i, 1].astype(BF16),
                  ffn_w_up[i, 1].astype(BF16), ffn_w_down[i, 1].astype(BF16), seq,
                  final_g=final_norm_g.reshape(1, d) if last else None)
    return x2.reshape(bsz, seq, d)
```

```python
import functools

import jax
import jax.numpy as jnp
from jax import lax
from jax.experimental import pallas as pl
from jax.experimental.pallas import tpu as pltpu

F32 = jnp.float32
BF16 = jnp.bfloat16
HIGHEST = lax.Precision.HIGHEST

RMS_EPS = 1e-6

SSM_HEAD_DIM = 64
SSM_GROUPS = 8
SSM_STATE = 128
SSM_CONV = 4
SSM_CHUNK = 256

ATTN_HEAD_DIM = 64
MOBA_BLOCK = 256
MOBA_TOPK = 3
ALIBI_MAX_BIAS = 8.0

LANES = 128
SUBLANES = 8

VMEM_LIMIT = 56 * 1024 * 1024
NEG_BIG = -(2.0 ** 100)
LOG2_E = 1.4426950408889634


def _nt(a, b, precision=None):
    return lax.dot_general(a, b, (((1,), (1,)), ((), ())),
                           preferred_element_type=F32, precision=precision)


def _mm(a, b, precision=None):
    return jnp.dot(a, b, preferred_element_type=F32, precision=precision)


def _silu(v):
    return v * jax.nn.sigmoid(v)


def _rms(x, g):
    ms = jnp.mean(x * x, axis=-1, keepdims=True)
    return x * lax.rsqrt(ms + RMS_EPS) * g


def _modulated(x, g, m, d):
    return _rms(x, g) * (1.0 + m[:, d:2 * d]) + m[:, :d]


def _resident(shape):
    return pl.BlockSpec(shape, lambda *_: (0,) * len(shape),
                        pipeline_mode=pl.Buffered(1))


def _params(n_axes):
    return pltpu.CompilerParams(dimension_semantics=("arbitrary",) * n_axes,
                                vmem_limit_bytes=VMEM_LIMIT)


def _mod_kernel(ct_ref, w_ref, b_ref, o_ref):
    ca = _silu(ct_ref[...])
    w = w_ref[...]
    rows = [jnp.sum(w * ca[:, b:b + 1], axis=0, keepdims=True)
            for b in range(ca.shape[1])]
    o_ref[...] = jnp.concatenate(rows, axis=0) + b_ref[...]


def _mod_params(c, mod_w, mod_b):
    bsz, d = c.shape
    n_sets = mod_w.shape[0] * mod_w.shape[1]
    w = mod_w.reshape(n_sets, d, 3 * d)
    b = mod_b.reshape(n_sets, 1, 3 * d)
    tn = 512
    return pl.pallas_call(
        _mod_kernel,
        grid=(n_sets, 3 * d // tn),
        in_specs=[pl.BlockSpec((d, bsz), lambda s, n: (0, 0)),
                  pl.BlockSpec((None, d, tn), lambda s, n: (s, 0, n)),
                  pl.BlockSpec((None, 1, tn), lambda s, n: (s, 0, n))],
        out_specs=pl.BlockSpec((None, bsz, tn), lambda s, n: (s, 0, n)),
        out_shape=jax.ShapeDtypeStruct((n_sets, bsz, 3 * d), F32),
        compiler_params=_params(2),
        name="adaln_params",
    )(c.T, w, b)


def _ffn_kernel(x_ref, m_ref, g_ref, wg_ref, wu_ref, wd_ref, *rest, tf, final):
    o_ref = rest[-1]
    d = x_ref.shape[1]
    x = x_ref[...]
    m = m_ref[...]
    h = _modulated(x, g_ref[...], m, d).astype(BF16)
    acc = None
    for f0 in range(0, wg_ref.shape[1], tf):
        a = _mm(h, wg_ref[:, f0:f0 + tf])
        u = _mm(h, wu_ref[:, f0:f0 + tf])
        act = (_silu(a) * u).astype(BF16)
        y = _mm(act, wd_ref[f0:f0 + tf, :])
        acc = y if acc is None else acc + y
    out = x + (0.5 * m[:, 2 * d:]) * acc
    if final:
        out = _rms(out, rest[0][...])
    o_ref[...] = out


def _ffn(x2, m, g, wg, wu, wd, rows_per_batch, final_g=None):
    t, d = x2.shape
    f = wg.shape[1]
    tm, tf = 512, 256
    per_b = rows_per_batch // tm
    in_specs = [pl.BlockSpec((tm, d), lambda i: (i, 0)),
                pl.BlockSpec((None, 1, 3 * d), lambda i: (i // per_b, 0, 0)),
                _resident((1, d)), _resident((d, f)), _resident((d, f)),
                _resident((f, d))]
    args = [x2, m, g, wg, wu, wd]
    if final_g is not None:
        in_specs.append(_resident((1, d)))
        args.append(final_g)
    return pl.pallas_call(
        functools.partial(_ffn_kernel, tf=tf, final=final_g is not None),
        grid=(t // tm,),
        in_specs=in_specs,
        out_specs=pl.BlockSpec((tm, d), lambda i: (i, 0)),
        out_shape=jax.ShapeDtypeStruct((t, d), F32),
        compiler_params=_params(1),
        name="swiglu_half_step",
    )(*args)


def _ssd_kernel(xres_ref, m_ref, g_ref, win_ref, wdt_ref, cw_ref, cb_ref,
                dtb_ref, alog_ref, dsk_ref, nw_ref, wo_ref, o_ref,
                xpad, xc, xs, z_s, y_s, state):
    q, d = xres_ref.shape
    conv_dim = xpad.shape[1]
    d_inner = z_s.shape[1]
    n_groups, gp, n_state = state.shape
    p = SSM_HEAD_DIM
    r_heads = gp // p
    taps = cw_ref.shape[0]
    pad = SUBLANES

    @pl.when(pl.program_id(1) == 0)
    def _():
        state[...] = jnp.zeros_like(state)
        xpad[0:pad, :] = jnp.zeros((pad, conv_dim), F32)

    m = m_ref[...]
    hin = _modulated(xres_ref[...], g_ref[...], m, d).astype(BF16)
    tn = 512
    for n0 in range(0, d_inner, tn):
        z_s[:, n0:n0 + tn] = _mm(hin, win_ref[:, n0:n0 + tn])
    for n0 in range(0, conv_dim, tn):
        xpad[pad:pad + q, n0:n0 + tn] = _mm(
            hin, win_ref[:, d_inner + n0:d_inner + n0 + tn])
    dt_raw = _nt(wdt_ref[...], hin)

    tc = 512
    for c0 in range(0, conv_dim, tc):
        acc = cb_ref[:, c0:c0 + tc] + (cw_ref[taps - 1:taps, c0:c0 + tc]
                                       * xpad[pad:pad + q, c0:c0 + tc])
        for k in range(taps - 1):
            r0 = pad - (taps - 1) + k
            acc = acc + cw_ref[k:k + 1, c0:c0 + tc] * xpad[r0:r0 + q, c0:c0 + tc]
        v = _silu(acc)
        xc[:, c0:c0 + tc] = v.astype(BF16)
        if c0 < d_inner:
            xs[:, c0:c0 + tc] = v
    xpad[0:pad, :] = xpad[q:q + pad, :]

    dt = jax.nn.softplus(dt_raw + dtb_ref[...])
    dta = dt * (-jnp.exp(alog_ref[...]))
    ri = lax.broadcasted_iota(jnp.int32, (q, q), 0)
    ci = lax.broadcasted_iota(jnp.int32, (q, q), 1)
    causal = ri >= ci
    acum_row = _mm(dta, (ri <= ci).astype(F32), HIGHEST)
    acum_col = _nt(causal.astype(F32), dta, HIGHEST)
    to_end = jnp.exp(acum_row[:, q - 1:q] - acum_row) * dt
    exp_acol = jnp.exp(acum_col)
    dec_last = exp_acol[q - 1:q, :]
    arow_dt = acum_row - jnp.log(dt)

    b0 = d_inner
    c0_ = d_inner + n_groups * n_state
    for g in range(n_groups):
        bg = xc[:, b0 + g * n_state:b0 + (g + 1) * n_state]
        cg = xc[:, c0_ + g * n_state:c0_ + (g + 1) * n_state]
        cb = _nt(cg, bg)
        st = state[g]
        y_int = _nt(cg, st.astype(BF16))
        ys, te, dec = [], [], []
        for r in range(r_heads):
            h = g * r_heads + r
            seg = acum_col[:, h:h + 1] - arow_dt[h:h + 1, :]
            w = (cb * jnp.exp(jnp.where(causal, seg, -jnp.inf))).astype(BF16)
            y_h = _mm(w, xc[:, h * p:(h + 1) * p])
            ys.append(y_h + y_int[:, r * p:(r + 1) * p] * exp_acol[:, h:h + 1])
            te.append(jnp.broadcast_to(to_end[h:h + 1, :], (p, q)))
            dec.append(jnp.broadcast_to(dec_last[:, h:h + 1], (p, n_state)))
        xg = xs[:, g * gp:(g + 1) * gp]
        y_s[:, g * gp:(g + 1) * gp] = (jnp.concatenate(ys, axis=1)
                                       + dsk_ref[:, g * gp:(g + 1) * gp] * xg)
        xw = (xg.T * jnp.concatenate(te, axis=0)).astype(BF16)
        state[g] = st * jnp.concatenate(dec, axis=0) + _mm(xw, bg)

    yg = y_s[...] * _silu(z_s[...])
    yn = _rms(yg, nw_ref[...]).astype(BF16)
    o_ref[...] = xres_ref[...] + m[:, 2 * d:] * _mm(yn, wo_ref[...])


def _ssd(x2, m, g, w_zx, w_dt_t, cw, cb, dtb, alog, dsk, nw, wo, bsz, seq):
    t, d = x2.shape
    d_inner = wo.shape[0]
    conv_dim = w_zx.shape[1] - d_inner
    q = SSM_CHUNK
    nc = seq // q
    gp = d_inner // SSM_GROUPS
    row = lambda b, c: (b * nc + c, 0)
    return pl.pallas_call(
        _ssd_kernel,
        grid=(bsz, nc),
        in_specs=[pl.BlockSpec((q, d), row),
                  pl.BlockSpec((None, 1, 3 * d), lambda b, c: (b, 0, 0)),
                  _resident(g.shape), _resident(w_zx.shape),
                  _resident(w_dt_t.shape),
                  _resident(cw.shape), _resident(cb.shape),
                  _resident(dtb.shape), _resident(alog.shape),
                  _resident(dsk.shape), _resident(nw.shape),
                  _resident(wo.shape)],
        out_specs=pl.BlockSpec((q, d), row),
        out_shape=jax.ShapeDtypeStruct((t, d), F32),
        scratch_shapes=[pltpu.VMEM((q + 2 * SUBLANES, conv_dim), F32),
                        pltpu.VMEM((q, conv_dim), BF16),
                        pltpu.VMEM((q, d_inner), F32),
                        pltpu.VMEM((q, d_inner), F32),
                        pltpu.VMEM((q, d_inner), F32),
                        pltpu.VMEM((SSM_GROUPS, gp, SSM_STATE), F32)],
        compiler_params=_params(2),
        name="mamba2_mixer",
    )(x2, m, g, w_zx, w_dt_t, cw, cb, dtb, alog, dsk, nw, wo)


KPOS_LANE = 96
SLOT = LANES


def _qkv_kernel(x_ref, m_ref, g_ref, wqt_ref, wk_ref, wvt_ref,
                qt_ref, k_ref, vt_ref, km_ref, *, per_b):
    tm, d = x_ref.shape
    blk = MOBA_BLOCK
    dh = ATTN_HEAD_DIM
    nbt = tm // blk
    scale = dh ** -0.5 * LOG2_E
    h = _modulated(x_ref[...], g_ref[...], m_ref[...], d).astype(BF16)

    tr = 256
    for r0 in range(0, d, tr):
        qt_ref[r0:r0 + tr, :] = (_nt(wqt_ref[r0:r0 + tr, :], h) * scale).astype(BF16)

    tn = 512
    row = lax.broadcasted_iota(jnp.int32, (tm, tn), 0)
    lane = lax.broadcasted_iota(jnp.int32, (tm, tn), 1) % SLOT
    blk_id = (pl.program_id(0) % per_b) * nbt + row // blk
    pos = (row % blk).astype(F32)
    extras = jnp.where(lane == dh + blk_id, 1.0,
                       jnp.where((lane >= KPOS_LANE) & (lane < KPOS_LANE + 3),
                                 pos, 0.0))
    for n0 in range(0, k_ref.shape[1], tn):
        kk = _mm(h, wk_ref[:, n0:n0 + tn])
        k_ref[:, n0:n0 + tn] = (kk + extras).astype(BF16)
        for j in range(nbt):
            km_ref[j, :, n0:n0 + tn] = jnp.mean(kk[j * blk:(j + 1) * blk, :],
                                                axis=0, keepdims=True)

    ones_row = lax.broadcasted_iota(jnp.int32, (tr, tm), 0) % SLOT == dh
    for r0 in range(0, vt_ref.shape[1], tr):
        vt = jnp.where(ones_row, 1.0, _nt(wvt_ref[r0:r0 + tr, :], h)).astype(BF16)
        for j in range(nbt):
            vt_ref[j, r0:r0 + tr, :] = vt[:, j * blk:(j + 1) * blk]


def _qkv(x2, m, g, wq_t, wk_pad, wv_t_pad, bsz, seq):
    t, d = x2.shape
    dp = wk_pad.shape[1]
    blk = MOBA_BLOCK
    tm = 512
    per_b = seq // tm
    nbt = tm // blk
    return pl.pallas_call(
        functools.partial(_qkv_kernel, per_b=per_b),
        grid=(t // tm,),
        in_specs=[pl.BlockSpec((tm, d), lambda i: (i, 0)),
                  pl.BlockSpec((None, 1, 3 * d), lambda i: (i // per_b, 0, 0)),
                  _resident((1, d)), _resident(wq_t.shape),
                  _resident(wk_pad.shape), _resident(wv_t_pad.shape)],
        out_specs=[pl.BlockSpec((None, d, tm), lambda i: (i // per_b, 0, i % per_b)),
                   pl.BlockSpec((tm, dp), lambda i: (i, 0)),
                   pl.BlockSpec((None, nbt, dp, blk),
                                lambda i: (i // per_b, i % per_b, 0, 0)),
                   pl.BlockSpec((nbt, 1, dp), lambda i: (i, 0, 0))],
        out_shape=[jax.ShapeDtypeStruct((bsz, d, seq), BF16),
                   jax.ShapeDtypeStruct((t, dp), BF16),
                   jax.ShapeDtypeStruct((bsz, seq // blk, dp, blk), BF16),
                   jax.ShapeDtypeStruct((t // blk, 1, dp), F32)],
        compiler_params=_params(1),
        name="moba_qkv_proj",
    )(x2, m, g, wq_t, wk_pad, wv_t_pad)


MOBA_HEADS_PER_STEP = 4


def _moba_kernel(sl_ref, qt_ref, k_ref, vt_ref, km_ref, aug_ref, o_ref, s_buf):
    hpg = aug_ref.shape[0]
    i = pl.program_id(2)
    blk = qt_ref.shape[1]
    dh = qt_ref.shape[0] // hpg
    nb = km_ref.shape[0]
    heads = range(hpg)
    step = [sl_ref[pl.program_id(1) * hpg + hh] * blk for hh in heads]

    bid = lax.broadcasted_iota(jnp.int32, (nb, blk), 0)
    qa, qa_own = [], []
    for hh in heads:
        qt = qt_ref[hh * dh:(hh + 1) * dh, :]
        km = km_ref[:, hh * SLOT:hh * SLOT + dh]
        hi = km.astype(BF16)
        lo = (km - hi.astype(F32)).astype(BF16)
        gate = _mm(hi, qt) + _mm(lo, qt)
        gate = jnp.where(bid < i, gate, -jnp.inf)
        bias = jnp.where(bid == i, 0.0, NEG_BIG)
        for _ in range(MOBA_TOPK):
            mx = jnp.max(gate, axis=0, keepdims=True)
            first = jnp.min(jnp.where(gate == mx, bid, nb), axis=0, keepdims=True)
            pick = (bid == first) & (mx > -jnp.inf)
            bias = jnp.where(pick, 0.0, bias)
            gate = jnp.where(pick, -jnp.inf, gate)
        qa.append(jnp.concatenate([qt, bias.astype(BF16), aug_ref[hh]], axis=0))
        qa_own.append(jnp.concatenate([qt, jnp.zeros((nb, blk), BF16), aug_ref[hh]],
                                      axis=0))

    kid = lax.broadcasted_iota(jnp.int32, (blk, blk), 0)
    qid = lax.broadcasted_iota(jnp.int32, (blk, blk), 1)
    causal = kid <= qid
    own0 = pl.multiple_of(i * blk, blk)

    def k_rows(hh, r0, rows):
        return k_ref[pl.ds(r0, rows), hh * SLOT:(hh + 1) * SLOT]

    def v_blk(hh, n):
        return vt_ref[n, hh * SLOT:(hh + 1) * SLOT, :]

    n_pairs = (i + 1) // 2
    last_pair = jnp.maximum(n_pairs - 1, 0)

    def stage_scores(buf, t):
        r0 = pl.multiple_of(jnp.minimum(t, last_pair) * (2 * blk), 2 * blk)
        tops = []
        for hh in heads:
            s = _mm(k_rows(hh, r0, 2 * blk), qa[hh])
            s_buf[buf, hh] = s
            tops.append((jnp.max(s[:blk], axis=0, keepdims=True),
                         jnp.max(s[blk:], axis=0, keepdims=True)))
        return tuple(tops)

    def absorb(buf, t, tops, state):
        n0 = 2 * t
        n1 = n0 + 1
        out = []
        for hh in heads:
            m, acc = state[hh]
            c0 = jnp.where(n0 < i, step[hh] * (n0 - i).astype(F32), NEG_BIG)
            c1 = jnp.where(n1 < i, step[hh] * (n1 - i).astype(F32), NEG_BIG)
            m_new = jnp.maximum(m, jnp.maximum(tops[hh][0] + c0, tops[hh][1] + c1))
            p0 = jnp.exp2(s_buf[buf, hh, :blk] - (m_new - c0)).astype(BF16)
            p1 = jnp.exp2(s_buf[buf, hh, blk:] - (m_new - c1)).astype(BF16)
            acc = (acc * jnp.exp2(m - m_new)
                   + _mm(v_blk(hh, jnp.minimum(n0, nb - 1)), p0)
                   + _mm(v_blk(hh, jnp.minimum(n1, nb - 1)), p1))
            out.append((m_new, acc))
        return tuple(out)

    s_own = [_mm(k_rows(hh, own0, blk), qa_own[hh]) for hh in heads]
    tops_a = stage_scores(0, 0)
    state = []
    for hh in heads:
        s = jnp.where(causal, s_own[hh], NEG_BIG)
        m = jnp.max(s, axis=0, keepdims=True)
        state.append((m, _mm(v_blk(hh, i), jnp.exp2(s - m).astype(BF16))))

    def visit_two_pairs(t2, carry):
        state, tops_a = carry
        t = 2 * t2
        tops_b = stage_scores(1, t + 1)
        state = absorb(0, t, tops_a, state)
        tops_a = stage_scores(0, t + 2)
        state = absorb(1, t + 1, tops_b, state)
        return state, tops_a

    fin, _ = lax.fori_loop(0, (n_pairs + 1) // 2, visit_two_pairs,
                           (tuple(state), tops_a))
    for hh in heads:
        acc = fin[hh][1]
        o_ref[hh * dh:(hh + 1) * dh, :] = (acc[:dh] / acc[dh:dh + 1]).astype(o_ref.dtype)


def _moba_attention(qt, kpad, vt, kmean, aug, slopes, bsz, seq):
    d = qt.shape[1]
    dh = ATTN_HEAD_DIM
    blk = MOBA_BLOCK
    nb = seq // blk
    hpg = MOBA_HEADS_PER_STEP
    n_groups = d // dh // hpg
    dp = kpad.shape[1]
    qspec = pl.BlockSpec((None, hpg * dh, blk), lambda b, h, i: (b, h, i))
    return pl.pallas_call(
        _moba_kernel,
        grid=(bsz, n_groups, nb),
        in_specs=[pl.BlockSpec(memory_space=pltpu.SMEM),
                  qspec,
                  pl.BlockSpec((None, seq, hpg * SLOT), lambda b, h, i: (b, 0, h)),
                  pl.BlockSpec((None, nb, hpg * SLOT, blk), lambda b, h, i: (b, 0, h, 0)),
                  pl.BlockSpec((None, nb, hpg * SLOT), lambda b, h, i: (b, 0, h)),
                  pl.BlockSpec((hpg, SLOT - dh - nb, blk), lambda b, h, i: (h, 0, 0))],
        out_specs=qspec,
        out_shape=jax.ShapeDtypeStruct((bsz, d, seq), BF16),
        scratch_shapes=[pltpu.VMEM((2, hpg, 2 * blk, blk), F32)],
        compiler_params=_params(3),
        name="moba_attention",
    )(slopes, qt, kpad.reshape(bsz, seq, dp), vt, kmean, aug)


def _proj_res_kernel(at_ref, x_ref, m_ref, w_ref, o_ref):
    d = x_ref.shape[1]
    y = lax.dot_general(at_ref[...], w_ref[...], (((0,), (0,)), ((), ())),
                        preferred_element_type=F32)
    o_ref[...] = x_ref[...] + m_ref[:, 2 * d:] * y


def _proj_res(at, x2, m, w, seq):
    t, d = x2.shape
    tm = 512
    per_b = seq // tm
    return pl.pallas_call(
        _proj_res_kernel,
        grid=(t // tm,),
        in_specs=[pl.BlockSpec((None, at.shape[1], tm),
                               lambda i: (i // per_b, 0, i % per_b)),
                  pl.BlockSpec((tm, d), lambda i: (i, 0)),
                  pl.BlockSpec((None, 1, 3 * d), lambda i: (i // per_b, 0, 0)),
                  _resident(w.shape)],
        out_specs=pl.BlockSpec((tm, d), lambda i: (i, 0)),
        out_shape=jax.ShapeDtypeStruct((t, d), F32),
        compiler_params=_params(1),
        name="attn_out_proj",
    )(at, x2, m, w)


def _moba_constants(d, seq):
    n_heads = d // ATTN_HEAD_DIM
    nb = seq // MOBA_BLOCK
    slopes = jnp.exp2(-ALIBI_MAX_BIAS * (jnp.arange(n_heads, dtype=F32) + 1.0) / n_heads)
    slopes = slopes * LOG2_E
    hi = slopes.astype(BF16)
    lo = (slopes - hi.astype(F32)).astype(BF16)
    lo2 = (slopes - hi.astype(F32) - lo.astype(F32)).astype(BF16)
    rows = SLOT - ATTN_HEAD_DIM - nb
    aug = jnp.zeros((n_heads, rows), BF16)
    first = KPOS_LANE - ATTN_HEAD_DIM - nb
    aug = aug.at[:, first].set(hi).at[:, first + 1].set(lo).at[:, first + 2].set(lo2)
    return slopes, jnp.broadcast_to(aug[:, :, None], (n_heads, rows, MOBA_BLOCK))


def _pad_heads(w, axis):
    shape = list(w.shape)
    n_heads = shape[axis] // ATTN_HEAD_DIM
    w = w.reshape(shape[:axis] + [n_heads, ATTN_HEAD_DIM] + shape[axis + 1:])
    pads = [(0, 0)] * w.ndim
    pads[axis + 1] = (0, SLOT - ATTN_HEAD_DIM)
    w = jnp.pad(w, pads)
    return w.reshape(shape[:axis] + [n_heads * SLOT] + shape[axis + 1:])


def kernel(x, c, norm_g, mod_w, mod_b, ffn_w_gate, ffn_w_up, ffn_w_down,
           ssm_w_in, ssm_conv_w, ssm_conv_b, ssm_dt_bias, ssm_a_log, ssm_d,
           ssm_norm_w, ssm_w_out, attn_w_qkv, attn_w_out, final_norm_g):
    bsz, seq, d = x.shape
    depth = norm_g.shape[0]
    d_inner = ssm_w_out.shape[1]
    heads = ssm_dt_bias.shape[1]
    assert seq // MOBA_BLOCK == SLOT - ATTN_HEAD_DIM - (SLOT - KPOS_LANE)

    m_all = _mod_params(c, mod_w, mod_b)
    x2 = x.reshape(bsz * seq, d)
    slopes, aug = _moba_constants(d, seq)

    for i in range(depth):
        ms = [m_all[3 * i + j].reshape(bsz, 1, 3 * d) for j in range(3)]
        gs = [norm_g[i, j].reshape(1, d) for j in range(3)]
        x2 = _ffn(x2, ms[0], gs[0], ffn_w_gate[i, 0].astype(BF16),
                  ffn_w_up[i, 0].astype(BF16), ffn_w_down[i, 0].astype(BF16), seq)
        j = i // 2
        if i % 2 == 0:
            n_zx = ssm_w_in.shape[2] - heads
            x2 = _ssd(x2, ms[1], gs[1],
                      ssm_w_in[j][:, :n_zx].astype(BF16),
                      ssm_w_in[j][:, n_zx:].T.astype(BF16),
                      ssm_conv_w[j].reshape(SSM_CONV, -1),
                      ssm_conv_b[j].reshape(1, -1),
                      ssm_dt_bias[j].reshape(heads, 1),
                      ssm_a_log[j].reshape(heads, 1),
                      jnp.repeat(ssm_d[j], SSM_HEAD_DIM).reshape(1, d_inner),
                      ssm_norm_w[j].reshape(1, d_inner),
                      ssm_w_out[j].astype(BF16), bsz, seq)
        else:
            wqkv = attn_w_qkv[j].astype(BF16)
            qt, kpad, vt, km = _qkv(x2, ms[1], gs[1], wqkv[:, :d].T,
                                    _pad_heads(wqkv[:, d:2 * d], 1),
                                    _pad_heads(wqkv[:, 2 * d:].T, 0), bsz, seq)
            att_t = _moba_attention(qt, kpad, vt,
                                    km.reshape(bsz, seq // MOBA_BLOCK, -1),
                                    aug, slopes, bsz, seq)
            x2 = _proj_res(att_t, x2, ms[1], attn_w_out[j].astype(BF16), seq)
        last = i == depth - 1
        x2 = _ffn(x2, ms[2], gs[2], ffn_w_gate[i, 1].astype(BF16),
                  ffn_w_up[i, 1].astype(BF16), ffn_w_down[i, 1].astype(BF16), seq,
                  final_g=final_norm_g.reshape(1, d) if last else None)
    return x2.reshape(bsz, seq, d)
```

```python
import functools

import jax
import jax.numpy as jnp
from jax import lax
from jax.experimental import pallas as pl
from jax.experimental.pallas import tpu as pltpu

F32 = jnp.float32
BF16 = jnp.bfloat16
HIGHEST = lax.Precision.HIGHEST

RMS_EPS = 1e-6

SSM_HEAD_DIM = 64
SSM_GROUPS = 8
SSM_STATE = 128
SSM_CONV = 4
SSM_CHUNK = 256

ATTN_HEAD_DIM = 64
MOBA_BLOCK = 256
MOBA_TOPK = 3
ALIBI_MAX_BIAS = 8.0

LANES = 128
SUBLANES = 8

VMEM_LIMIT = 56 * 1024 * 1024
NEG_BIG = -(2.0 ** 100)
LOG2_E = 1.4426950408889634


def _nt(a, b, precision=None):
    return lax.dot_general(a, b, (((1,), (1,)), ((), ())),
                           preferred_element_type=F32, precision=precision)


def _mm(a, b, precision=None):
    return jnp.dot(a, b, preferred_element_type=F32, precision=precision)


def _silu(v):
    return v * jax.nn.sigmoid(v)


def _rms(x, g):
    ms = jnp.mean(x * x, axis=-1, keepdims=True)
    return x * lax.rsqrt(ms + RMS_EPS) * g


def _modulated(x, g, m, d):
    return _rms(x, g) * (1.0 + m[:, d:2 * d]) + m[:, :d]


def _resident(shape):
    return pl.BlockSpec(shape, lambda *_: (0,) * len(shape),
                        pipeline_mode=pl.Buffered(1))


def _params(n_axes):
    return pltpu.CompilerParams(dimension_semantics=("arbitrary",) * n_axes,
                                vmem_limit_bytes=VMEM_LIMIT)


def _mod_kernel(ct_ref, w_ref, b_ref, o_ref):
    ca = _silu(ct_ref[...])
    w = w_ref[...]
    rows = [jnp.sum(w * ca[:, b:b + 1], axis=0, keepdims=True)
            for b in range(ca.shape[1])]
    o_ref[...] = jnp.concatenate(rows, axis=0) + b_ref[...]


def _mod_params(c, mod_w, mod_b):
    bsz, d = c.shape
    n_sets = mod_w.shape[0] * mod_w.shape[1]
    w = mod_w.reshape(n_sets, d, 3 * d)
    b = mod_b.reshape(n_sets, 1, 3 * d)
    tn = 512
    return pl.pallas_call(
        _mod_kernel,
        grid=(n_sets, 3 * d // tn),
        in_specs=[pl.BlockSpec((d, bsz), lambda s, n: (0, 0)),
                  pl.BlockSpec((None, d, tn), lambda s, n: (s, 0, n)),
                  pl.BlockSpec((None, 1, tn), lambda s, n: (s, 0, n))],
        out_specs=pl.BlockSpec((None, bsz, tn), lambda s, n: (s, 0, n)),
        out_shape=jax.ShapeDtypeStruct((n_sets, bsz, 3 * d), F32),
        compiler_params=_params(2),
        name="adaln_params",
    )(c.T, w, b)


def _ffn_kernel(x_ref, m_ref, g_ref, wg_ref, wu_ref, wd_ref, *rest, tf, final):
    o_ref = rest[-1]
    d = x_ref.shape[1]
    x = x_ref[...]
    m = m_ref[...]
    h = _modulated(x, g_ref[...], m, d).astype(BF16)
    acc = None
    for f0 in range(0, wg_ref.shape[1], tf):
        a = _mm(h, wg_ref[:, f0:f0 + tf])
        u = _mm(h, wu_ref[:, f0:f0 + tf])
        act = (_silu(a) * u).astype(BF16)
        y = _mm(act, wd_ref[f0:f0 + tf, :])
        acc = y if acc is None else acc + y
    out = x + (0.5 * m[:, 2 * d:]) * acc
    if final:
        out = _rms(out, rest[0][...])
    o_ref[...] = out


def _ffn(x2, m, g, wg, wu, wd, rows_per_batch, final_g=None):
    t, d = x2.shape
    f = wg.shape[1]
    tm, tf = 512, 256
    per_b = rows_per_batch // tm
    in_specs = [pl.BlockSpec((tm, d), lambda i: (i, 0)),
                pl.BlockSpec((None, 1, 3 * d), lambda i: (i // per_b, 0, 0)),
                _resident((1, d)), _resident((d, f)), _resident((d, f)),
                _resident((f, d))]
    args = [x2, m, g, wg, wu, wd]
    if final_g is not None:
        in_specs.append(_resident((1, d)))
        args.append(final_g)
    return pl.pallas_call(
        functools.partial(_ffn_kernel, tf=tf, final=final_g is not None),
        grid=(t // tm,),
        in_specs=in_specs,
        out_specs=pl.BlockSpec((tm, d), lambda i: (i, 0)),
        out_shape=jax.ShapeDtypeStruct((t, d), F32),
        compiler_params=_params(1),
        name="swiglu_half_step",
    )(*args)


def _ssd_kernel(xres_ref, m_ref, g_ref, win_ref, wdt_ref, cw_ref, cb_ref,
                dtb_ref, alog_ref, dsk_ref, nw_ref, wo_ref, o_ref,
                xpad, xc, xs, z_s, y_s, state):
    q, d = xres_ref.shape
    conv_dim = xpad.shape[1]
    d_inner = z_s.shape[1]
    n_groups, gp, n_state = state.shape
    p = SSM_HEAD_DIM
    r_heads = gp // p
    taps = cw_ref.shape[0]
    pad = SUBLANES

    @pl.when(pl.program_id(1) == 0)
    def _():
        state[...] = jnp.zeros_like(state)
        xpad[0:pad, :] = jnp.zeros((pad, conv_dim), F32)

    m = m_ref[...]
    hin = _modulated(xres_ref[...], g_ref[...], m, d).astype(BF16)
    dt_raw = _nt(wdt_ref[...], hin)

    tc = 512
    for c0 in range(0, conv_dim, tc):
        xpad[pad:pad + q, c0:c0 + tc] = _mm(
            hin, win_ref[:, d_inner + c0:d_inner + c0 + tc])
        acc = cb_ref[:, c0:c0 + tc] + (cw_ref[taps - 1:taps, c0:c0 + tc]
                                       * xpad[pad:pad + q, c0:c0 + tc])
        for k in range(taps - 1):
            r0 = pad - (taps - 1) + k
            acc = acc + cw_ref[k:k + 1, c0:c0 + tc] * xpad[r0:r0 + q, c0:c0 + tc]
        v = _silu(acc)
        xc[:, c0:c0 + tc] = v.astype(BF16)
        if c0 < d_inner:
            xs[:, c0:c0 + tc] = v
    xpad[0:pad, :] = xpad[q:q + pad, :]
    for n0 in range(0, d_inner, tc):
        z_s[:, n0:n0 + tc] = _mm(hin, win_ref[:, n0:n0 + tc])

    dt = jax.nn.softplus(dt_raw + dtb_ref[...])
    dta = dt * (-jnp.exp(alog_ref[...]))
    ri = lax.broadcasted_iota(jnp.int32, (q, q), 0)
    ci = lax.broadcasted_iota(jnp.int32, (q, q), 1)
    causal = ri >= ci
    acum_row = _mm(dta, (ri <= ci).astype(F32), HIGHEST)
    acum_col = _nt(causal.astype(F32), dta, HIGHEST)
    to_end = jnp.exp(acum_row[:, q - 1:q] - acum_row) * dt
    exp_acol = jnp.exp(acum_col)
    dec_last = exp_acol[q - 1:q, :]
    arow_dt = acum_row - jnp.log(dt)

    b0 = d_inner
    c0_ = d_inner + n_groups * n_state
    for g in range(n_groups):
        bg = xc[:, b0 + g * n_state:b0 + (g + 1) * n_state]
        cg = xc[:, c0_ + g * n_state:c0_ + (g + 1) * n_state]
        cb = _nt(cg, bg)
        st = state[g]
        y_int = _nt(cg, st.astype(BF16))
        ys, te, dec = [], [], []
        for r in range(r_heads):
            h = g * r_heads + r
            seg = acum_col[:, h:h + 1] - arow_dt[h:h + 1, :]
            w = (cb * jnp.exp(jnp.where(causal, seg, -jnp.inf))).astype(BF16)
            y_h = _mm(w, xc[:, h * p:(h + 1) * p])
            ys.append(y_h + y_int[:, r * p:(r + 1) * p] * exp_acol[:, h:h + 1])
            te.append(jnp.broadcast_to(to_end[h:h + 1, :], (p, q)))
            dec.append(jnp.broadcast_to(dec_last[:, h:h + 1], (p, n_state)))
        xg = xs[:, g * gp:(g + 1) * gp]
        y_s[:, g * gp:(g + 1) * gp] = (jnp.concatenate(ys, axis=1)
                                       + dsk_ref[:, g * gp:(g + 1) * gp] * xg)
        xw = (xg.T * jnp.concatenate(te, axis=0)).astype(BF16)
        state[g] = st * jnp.concatenate(dec, axis=0) + _mm(xw, bg)

    yg = y_s[...] * _silu(z_s[...])
    yn = _rms(yg, nw_ref[...]).astype(BF16)
    o_ref[...] = xres_ref[...] + m[:, 2 * d:] * _mm(yn, wo_ref[...])


def _ssd(x2, m, g, w_zx, w_dt_t, cw, cb, dtb, alog, dsk, nw, wo, bsz, seq):
    t, d = x2.shape
    d_inner = wo.shape[0]
    conv_dim = w_zx.shape[1] - d_inner
    q = SSM_CHUNK
    nc = seq // q
    gp = d_inner // SSM_GROUPS
    row = lambda b, c: (b * nc + c, 0)
    return pl.pallas_call(
        _ssd_kernel,
        grid=(bsz, nc),
        in_specs=[pl.BlockSpec((q, d), row),
                  pl.BlockSpec((None, 1, 3 * d), lambda b, c: (b, 0, 0)),
                  _resident(g.shape), _resident(w_zx.shape),
                  _resident(w_dt_t.shape),
                  _resident(cw.shape), _resident(cb.shape),
                  _resident(dtb.shape), _resident(alog.shape),
                  _resident(dsk.shape), _resident(nw.shape),
                  _resident(wo.shape)],
        out_specs=pl.BlockSpec((q, d), row),
        out_shape=jax.ShapeDtypeStruct((t, d), F32),
        scratch_shapes=[pltpu.VMEM((q + 2 * SUBLANES, conv_dim), F32),
                        pltpu.VMEM((q, conv_dim), BF16),
                        pltpu.VMEM((q, d_inner), F32),
                        pltpu.VMEM((q, d_inner), F32),
                        pltpu.VMEM((q, d_inner), F32),
                        pltpu.VMEM((SSM_GROUPS, gp, SSM_STATE), F32)],
        compiler_params=_params(2),
        name="mamba2_mixer",
    )(x2, m, g, w_zx, w_dt_t, cw, cb, dtb, alog, dsk, nw, wo)


KPOS_LANE = 96
SLOT = LANES


def _qkv_kernel(x_ref, m_ref, g_ref, wqt_ref, wk_ref, wvt_ref,
                qt_ref, k_ref, vt_ref, km_ref, *, per_b):
    tm, d = x_ref.shape
    blk = MOBA_BLOCK
    dh = ATTN_HEAD_DIM
    nbt = tm // blk
    scale = dh ** -0.5 * LOG2_E
    h = _modulated(x_ref[...], g_ref[...], m_ref[...], d).astype(BF16)

    tr = 256
    for r0 in range(0, d, tr):
        qt_ref[r0:r0 + tr, :] = (_nt(wqt_ref[r0:r0 + tr, :], h) * scale).astype(BF16)

    tn = 512
    row = lax.broadcasted_iota(jnp.int32, (tm, tn), 0)
    lane = lax.broadcasted_iota(jnp.int32, (tm, tn), 1) % SLOT
    blk_id = (pl.program_id(0) % per_b) * nbt + row // blk
    pos = (row % blk).astype(F32)
    extras = jnp.where(lane == dh + blk_id, 1.0,
                       jnp.where((lane >= KPOS_LANE) & (lane < KPOS_LANE + 3),
                                 pos, 0.0))
    for n0 in range(0, k_ref.shape[1], tn):
        kk = _mm(h, wk_ref[:, n0:n0 + tn])
        k_ref[:, n0:n0 + tn] = (kk + extras).astype(BF16)
        for j in range(nbt):
            km_ref[j, :, n0:n0 + tn] = jnp.mean(kk[j * blk:(j + 1) * blk, :],
                                                axis=0, keepdims=True)

    ones_row = lax.broadcasted_iota(jnp.int32, (tr, tm), 0) % SLOT == dh
    for r0 in range(0, vt_ref.shape[1], tr):
        vt = jnp.where(ones_row, 1.0, _nt(wvt_ref[r0:r0 + tr, :], h)).astype(BF16)
        for j in range(nbt):
            vt_ref[j, r0:r0 + tr, :] = vt[:, j * blk:(j + 1) * blk]


def _qkv(x2, m, g, wq_t, wk_pad, wv_t_pad, bsz, seq):
    t, d = x2.shape
    dp = wk_pad.shape[1]
    blk = MOBA_BLOCK
    tm = 512
    per_b = seq // tm
    nbt = tm // blk
    return pl.pallas_call(
        functools.partial(_qkv_kernel, per_b=per_b),
        grid=(t // tm,),
        in_specs=[pl.BlockSpec((tm, d), lambda i: (i, 0)),
                  pl.BlockSpec((None, 1, 3 * d), lambda i: (i // per_b, 0, 0)),
                  _resident((1, d)), _resident(wq_t.shape),
                  _resident(wk_pad.shape), _resident(wv_t_pad.shape)],
        out_specs=[pl.BlockSpec((None, d, tm), lambda i: (i // per_b, 0, i % per_b)),
                   pl.BlockSpec((tm, dp), lambda i: (i, 0)),
                   pl.BlockSpec((None, nbt, dp, blk),
                                lambda i: (i // per_b, i % per_b, 0, 0)),
                   pl.BlockSpec((nbt, 1, dp), lambda i: (i, 0, 0))],
        out_shape=[jax.ShapeDtypeStruct((bsz, d, seq), BF16),
                   jax.ShapeDtypeStruct((t, dp), BF16),
                   jax.ShapeDtypeStruct((bsz, seq // blk, dp, blk), BF16),
                   jax.ShapeDtypeStruct((t // blk, 1, dp), F32)],
        compiler_params=_params(1),
        name="moba_qkv_proj",
    )(x2, m, g, wq_t, wk_pad, wv_t_pad)


MOBA_HEADS_PER_STEP = 8


def _moba_kernel(sl_ref, qt_ref, k_ref, vt_ref, km_ref, aug_ref, o_ref, s_buf):
    hpg = aug_ref.shape[0]
    i = pl.program_id(2)
    blk = qt_ref.shape[1]
    dh = qt_ref.shape[0] // hpg
    nb = km_ref.shape[0]
    heads = range(hpg)
    step = [sl_ref[pl.program_id(1) * hpg + hh] * blk for hh in heads]

    bid = lax.broadcasted_iota(jnp.int32, (nb, blk), 0)
    qa, qa_own = [], []
    for hh in heads:
        qt = qt_ref[hh * dh:(hh + 1) * dh, :]
        km = km_ref[:, hh * SLOT:hh * SLOT + dh]
        hi = km.astype(BF16)
        lo = (km - hi.astype(F32)).astype(BF16)
        gate = _mm(hi, qt) + _mm(lo, qt)
        gate = jnp.where(bid < i, gate, -jnp.inf)
        bias = jnp.where(bid == i, 0.0, NEG_BIG)
        for _ in range(MOBA_TOPK):
            mx = jnp.max(gate, axis=0, keepdims=True)
            first = jnp.min(jnp.where(gate == mx, bid, nb), axis=0, keepdims=True)
            pick = (bid == first) & (mx > -jnp.inf)
            bias = jnp.where(pick, 0.0, bias)
            gate = jnp.where(pick, -jnp.inf, gate)
        qa.append(jnp.concatenate([qt, bias.astype(BF16), aug_ref[hh]], axis=0))
        qa_own.append(jnp.concatenate([qt, jnp.zeros((nb, blk), BF16), aug_ref[hh]],
                                      axis=0))

    kid = lax.broadcasted_iota(jnp.int32, (blk, blk), 0)
    qid = lax.broadcasted_iota(jnp.int32, (blk, blk), 1)
    causal = kid <= qid
    own0 = pl.multiple_of(i * blk, blk)

    def k_rows(hh, r0, rows):
        return k_ref[pl.ds(r0, rows), hh * SLOT:(hh + 1) * SLOT]

    def v_blk(hh, n):
        return vt_ref[n, hh * SLOT:(hh + 1) * SLOT, :]

    n_pairs = (i + 1) // 2
    last_pair = jnp.maximum(n_pairs - 1, 0)

    def stage_scores(buf, t):
        r0 = pl.multiple_of(jnp.minimum(t, last_pair) * (2 * blk), 2 * blk)
        tops = []
        for hh in heads:
            s = _mm(k_rows(hh, r0, 2 * blk), qa[hh])
            s_buf[buf, hh] = s
            tops.append((jnp.max(s[:blk], axis=0, keepdims=True),
                         jnp.max(s[blk:], axis=0, keepdims=True)))
        return tuple(tops)

    def absorb(buf, t, tops, state):
        n0 = 2 * t
        n1 = n0 + 1
        out = []
        for hh in heads:
            m, acc = state[hh]
            c0 = jnp.where(n0 < i, step[hh] * (n0 - i).astype(F32), NEG_BIG)
            c1 = jnp.where(n1 < i, step[hh] * (n1 - i).astype(F32), NEG_BIG)
            m_new = jnp.maximum(m, jnp.maximum(tops[hh][0] + c0, tops[hh][1] + c1))
            p0 = jnp.exp2(s_buf[buf, hh, :blk] - (m_new - c0)).astype(BF16)
            p1 = jnp.exp2(s_buf[buf, hh, blk:] - (m_new - c1)).astype(BF16)
            acc = (acc * jnp.exp2(m - m_new)
                   + _mm(v_blk(hh, jnp.minimum(n0, nb - 1)), p0)
                   + _mm(v_blk(hh, jnp.minimum(n1, nb - 1)), p1))
            out.append((m_new, acc))
        return tuple(out)

    s_own = [_mm(k_rows(hh, own0, blk), qa_own[hh]) for hh in heads]
    tops_a = stage_scores(0, 0)
    state = []
    for hh in heads:
        s = jnp.where(causal, s_own[hh], NEG_BIG)
        m = jnp.max(s, axis=0, keepdims=True)
        state.append((m, _mm(v_blk(hh, i), jnp.exp2(s - m).astype(BF16))))

    def visit_two_pairs(t2, carry):
        state, tops_a = carry
        t = 2 * t2
        tops_b = stage_scores(1, t + 1)
        state = absorb(0, t, tops_a, state)
        tops_a = stage_scores(0, t + 2)
        state = absorb(1, t + 1, tops_b, state)
        return state, tops_a

    fin, _ = lax.fori_loop(0, (n_pairs + 1) // 2, visit_two_pairs,
                           (tuple(state), tops_a))
    for hh in heads:
        acc = fin[hh][1]
        o_ref[hh * dh:(hh + 1) * dh, :] = (acc[:dh] / acc[dh:dh + 1]).astype(o_ref.dtype)


def _moba_attention(qt, kpad, vt, kmean, aug, slopes, bsz, seq):
    d = qt.shape[1]
    dh = ATTN_HEAD_DIM
    blk = MOBA_BLOCK
    nb = seq // blk
    hpg = MOBA_HEADS_PER_STEP
    n_groups = d // dh // hpg
    dp = kpad.shape[1]
    qspec = pl.BlockSpec((None, hpg * dh, blk), lambda b, h, i: (b, h, i))
    return pl.pallas_call(
        _moba_kernel,
        grid=(bsz, n_groups, nb),
        in_specs=[pl.BlockSpec(memory_space=pltpu.SMEM),
                  qspec,
                  pl.BlockSpec((None, seq, hpg * SLOT), lambda b, h, i: (b, 0, h),
                               pipeline_mode=pl.Buffered(1)),
                  pl.BlockSpec((None, nb, hpg * SLOT, blk), lambda b, h, i: (b, 0, h, 0),
                               pipeline_mode=pl.Buffered(1)),
                  pl.BlockSpec((None, nb, hpg * SLOT), lambda b, h, i: (b, 0, h)),
                  pl.BlockSpec((hpg, SLOT - dh - nb, blk), lambda b, h, i: (h, 0, 0))],
        out_specs=qspec,
        out_shape=jax.ShapeDtypeStruct((bsz, d, seq), BF16),
        scratch_shapes=[pltpu.VMEM((2, hpg, 2 * blk, blk), F32)],
        compiler_params=_params(3),
        name="moba_attention",
    )(slopes, qt, kpad.reshape(bsz, seq, dp), vt, kmean, aug)


def _proj_res_kernel(at_ref, x_ref, m_ref, w_ref, o_ref):
    d = x_ref.shape[1]
    y = lax.dot_general(at_ref[...], w_ref[...], (((0,), (0,)), ((), ())),
                        preferred_element_type=F32)
    o_ref[...] = x_ref[...] + m_ref[:, 2 * d:] * y


def _proj_res(at, x2, m, w, seq):
    t, d = x2.shape
    tm = 512
    per_b = seq // tm
    return pl.pallas_call(
        _proj_res_kernel,
        grid=(t // tm,),
        in_specs=[pl.BlockSpec((None, at.shape[1], tm),
                               lambda i: (i // per_b, 0, i % per_b)),
                  pl.BlockSpec((tm, d), lambda i: (i, 0)),
                  pl.BlockSpec((None, 1, 3 * d), lambda i: (i // per_b, 0, 0)),
                  _resident(w.shape)],
        out_specs=pl.BlockSpec((tm, d), lambda i: (i, 0)),
        out_shape=jax.ShapeDtypeStruct((t, d), F32),
        compiler_params=_params(1),
        name="attn_out_proj",
    )(at, x2, m, w)


def _moba_constants(d, seq):
    n_heads = d // ATTN_HEAD_DIM
    nb = seq // MOBA_BLOCK
    slopes = jnp.exp2(-ALIBI_MAX_BIAS * (jnp.arange(n_heads, dtype=F32) + 1.0) / n_heads)
    slopes = slopes * LOG2_E
    hi = slopes.astype(BF16)
    lo = (slopes - hi.astype(F32)).astype(BF16)
    lo2 = (slopes - hi.astype(F32) - lo.astype(F32)).astype(BF16)
    rows = SLOT - ATTN_HEAD_DIM - nb
    aug = jnp.zeros((n_heads, rows), BF16)
    first = KPOS_LANE - ATTN_HEAD_DIM - nb
    aug = aug.at[:, first].set(hi).at[:, first + 1].set(lo).at[:, first + 2].set(lo2)
    return slopes, jnp.broadcast_to(aug[:, :, None], (n_heads, rows, MOBA_BLOCK))


def _pad_heads(w, axis):
    shape = list(w.shape)
    n_heads = shape[axis] // ATTN_HEAD_DIM
    w = w.reshape(shape[:axis] + [n_heads, ATTN_HEAD_DIM] + shape[axis + 1:])
    pads = [(0, 0)] * w.ndim
    pads[axis + 1] = (0, SLOT - ATTN_HEAD_DIM)
    w = jnp.pad(w, pads)
    return w.reshape(shape[:axis] + [n_heads * SLOT] + shape[axis + 1:])


def kernel(x, c, norm_g, mod_w, mod_b, ffn_w_gate, ffn_w_up, ffn_w_down,
           ssm_w_in, ssm_conv_w, ssm_conv_b, ssm_dt_bias, ssm_a_log, ssm_d,
           ssm_norm_w, ssm_w_out, attn_w_qkv, attn_w_out, final_norm_g):
    bsz, seq, d = x.shape
    depth = norm_g.shape[0]
    d_inner = ssm_w_out.shape[1]
    heads = ssm_dt_bias.shape[1]
    assert seq // MOBA_BLOCK == SLOT - ATTN_HEAD_DIM - (SLOT - KPOS_LANE)

    m_all = _mod_params(c, mod_w, mod_b)
    x2 = x.reshape(bsz * seq, d)
    slopes, aug = _moba_constants(d, seq)

    for i in range(depth):
        ms = [m_all[3 * i + j].reshape(bsz, 1, 3 * d) for j in range(3)]
        gs = [norm_g[i, j].reshape(1, d) for j in range(3)]
        x2 = _ffn(x2, ms[0], gs[0], ffn_w_gate[i, 0].astype(BF16),
                  ffn_w_up[i, 0].astype(BF16), ffn_w_down[i, 0].astype(BF16), seq)
        j = i // 2
        if i % 2 == 0:
            n_zx = ssm_w_in.shape[2] - heads
            x2 = _ssd(x2, ms[1], gs[1],
                      ssm_w_in[j][:, :n_zx].astype(BF16),
                      ssm_w_in[j][:, n_zx:].T.astype(BF16),
                      ssm_conv_w[j].reshape(SSM_CONV, -1),
                      ssm_conv_b[j].reshape(1, -1),
                      ssm_dt_bias[j].reshape(heads, 1),
                      ssm_a_log[j].reshape(heads, 1),
                      jnp.repeat(ssm_d[j], SSM_HEAD_DIM).reshape(1, d_inner),
                      ssm_norm_w[j].reshape(1, d_inner),
                      ssm_w_out[j].astype(BF16), bsz, seq)
        else:
            wqkv = attn_w_qkv[j].astype(BF16)
            qt, kpad, vt, km = _qkv(x2, ms[1], gs[1], wqkv[:, :d].T,
                                    _pad_heads(wqkv[:, d:2 * d], 1),
                                    _pad_heads(wqkv[:, 2 * d:].T, 0), bsz, seq)
            att_t = _moba_attention(qt, kpad, vt,
                                    km.reshape(bsz, seq // MOBA_BLOCK, -1),
                                    aug, slopes, bsz, seq)
            x2 = _proj_res(att_t, x2, ms[1], attn_w_out[j].astype(BF16), seq)
        last = i == depth - 1
        x2 = _ffn(x2, ms[2], gs[2], ffn_w_gate[i, 1].astype(BF16),
                  ffn_w_up[i, 1].astype(BF16), ffn_w_down[i, 1].astype(BF16), seq,
                  final_g=final_norm_g.reshape(1, d) if last else None)
    return x2.reshape(bsz, seq, d)
```

```python
import functools

import jax
import jax.numpy as jnp
from jax import lax
from jax.experimental import pallas as pl
from jax.experimental.pallas import tpu as pltpu

F32 = jnp.float32
BF16 = jnp.bfloat16
HIGHEST = lax.Precision.HIGHEST

RMS_EPS = 1e-6

SSM_HEAD_DIM = 64
SSM_GROUPS = 8
SSM_STATE = 128
SSM_CONV = 4
SSM_CHUNK = 256

ATTN_HEAD_DIM = 64
MOBA_BLOCK = 256
MOBA_TOPK = 3
ALIBI_MAX_BIAS = 8.0

LANES = 128
SUBLANES = 8

VMEM_LIMIT = 56 * 1024 * 1024
NEG_BIG = -(2.0 ** 100)
LOG2_E = 1.4426950408889634


def _nt(a, b, precision=None):
    return lax.dot_general(a, b, (((1,), (1,)), ((), ())),
                           preferred_element_type=F32, precision=precision)


def _mm(a, b, precision=None):
    return jnp.dot(a, b, preferred_element_type=F32, precision=precision)


def _silu(v):
    return v * jax.nn.sigmoid(v)


def _rms(x, g):
    ms = jnp.mean(x * x, axis=-1, keepdims=True)
    return x * lax.rsqrt(ms + RMS_EPS) * g


def _modulated(x, g, m, d):
    return _rms(x, g) * (1.0 + m[:, d:2 * d]) + m[:, :d]


def _resident(shape):
    return pl.BlockSpec(shape, lambda *_: (0,) * len(shape),
                        pipeline_mode=pl.Buffered(1))


def _params(n_axes):
    return pltpu.CompilerParams(dimension_semantics=("arbitrary",) * n_axes,
                                vmem_limit_bytes=VMEM_LIMIT)


def _mod_kernel(ct_ref, w_ref, b_ref, o_ref):
    ca = _silu(ct_ref[...])
    w = w_ref[...]
    rows = [jnp.sum(w * ca[:, b:b + 1], axis=0, keepdims=True)
            for b in range(ca.shape[1])]
    o_ref[...] = jnp.concatenate(rows, axis=0) + b_ref[...]


def _mod_params(c, mod_w, mod_b):
    bsz, d = c.shape
    n_sets = mod_w.shape[0] * mod_w.shape[1]
    w = mod_w.reshape(n_sets, d, 3 * d)
    b = mod_b.reshape(n_sets, 1, 3 * d)
    tn = 512
    return pl.pallas_call(
        _mod_kernel,
        grid=(n_sets, 3 * d // tn),
        in_specs=[pl.BlockSpec((d, bsz), lambda s, n: (0, 0)),
                  pl.BlockSpec((None, d, tn), lambda s, n: (s, 0, n)),
                  pl.BlockSpec((None, 1, tn), lambda s, n: (s, 0, n))],
        out_specs=pl.BlockSpec((None, bsz, tn), lambda s, n: (s, 0, n)),
        out_shape=jax.ShapeDtypeStruct((n_sets, bsz, 3 * d), F32),
        compiler_params=_params(2),
        name="adaln_params",
    )(c.T, w, b)


def _ffn_kernel(x_ref, m_ref, g_ref, wg_ref, wu_ref, wd_ref, *rest, tf, final):
    o_ref = rest[-1]
    d = x_ref.shape[1]
    x = x_ref[...]
    m = m_ref[...]
    h = _modulated(x, g_ref[...], m, d).astype(BF16)
    acc = None
    for f0 in range(0, wg_ref.shape[1], tf):
        a = _mm(h, wg_ref[:, f0:f0 + tf])
        u = _mm(h, wu_ref[:, f0:f0 + tf])
        act = (_silu(a) * u).astype(BF16)
        y = _mm(act, wd_ref[f0:f0 + tf, :])
        acc = y if acc is None else acc + y
    out = x + (0.5 * m[:, 2 * d:]) * acc
    if final:
        out = _rms(out, rest[0][...])
    o_ref[...] = out


def _ffn(x2, m, g, wg, wu, wd, rows_per_batch, final_g=None):
    t, d = x2.shape
    f = wg.shape[1]
    tm, tf = 512, 256
    per_b = rows_per_batch // tm
    in_specs = [pl.BlockSpec((tm, d), lambda i: (i, 0)),
                pl.BlockSpec((None, 1, 3 * d), lambda i: (i // per_b, 0, 0)),
                _resident((1, d)), _resident((d, f)), _resident((d, f)),
                _resident((f, d))]
    args = [x2, m, g, wg, wu, wd]
    if final_g is not None:
        in_specs.append(_resident((1, d)))
        args.append(final_g)
    return pl.pallas_call(
        functools.partial(_ffn_kernel, tf=tf, final=final_g is not None),
        grid=(t // tm,),
        in_specs=in_specs,
        out_specs=pl.BlockSpec((tm, d), lambda i: (i, 0)),
        out_shape=jax.ShapeDtypeStruct((t, d), F32),
        compiler_params=_params(1),
        name="swiglu_half_step",
    )(*args)


def _ssd_kernel(xres_ref, m_ref, g_ref, win_ref, wdt_ref, cw_ref, cb_ref,
                dtb_ref, alog_ref, dsk_ref, nw_ref, wo_ref, o_ref,
                xpad, xc, xs, z_s, y_s, state):
    q, d = xres_ref.shape
    conv_dim = xpad.shape[1]
    d_inner = z_s.shape[1]
    n_groups, gp, n_state = state.shape
    p = SSM_HEAD_DIM
    r_heads = gp // p
    taps = cw_ref.shape[0]
    pad = SUBLANES

    @pl.when(pl.program_id(1) == 0)
    def _():
        state[...] = jnp.zeros_like(state)
        xpad[0:pad, :] = jnp.zeros((pad, conv_dim), F32)

    m = m_ref[...]
    hin = _modulated(xres_ref[...], g_ref[...], m, d).astype(BF16)
    dt_raw = _nt(wdt_ref[...], hin)

    tc = 512
    for c0 in range(0, conv_dim, tc):
        xpad[pad:pad + q, c0:c0 + tc] = _mm(
            hin, win_ref[:, d_inner + c0:d_inner + c0 + tc])
        acc = cb_ref[:, c0:c0 + tc] + (cw_ref[taps - 1:taps, c0:c0 + tc]
                                       * xpad[pad:pad + q, c0:c0 + tc])
        for k in range(taps - 1):
            r0 = pad - (taps - 1) + k
            acc = acc + cw_ref[k:k + 1, c0:c0 + tc] * xpad[r0:r0 + q, c0:c0 + tc]
        v = _silu(acc)
        xc[:, c0:c0 + tc] = v.astype(BF16)
        if c0 < d_inner:
            xs[:, c0:c0 + tc] = v
    xpad[0:pad, :] = xpad[q:q + pad, :]
    for n0 in range(0, d_inner, tc):
        z_s[:, n0:n0 + tc] = _mm(hin, win_ref[:, n0:n0 + tc])

    dt = jax.nn.softplus(dt_raw + dtb_ref[...])
    dta = dt * (-jnp.exp(alog_ref[...]))
    ri = lax.broadcasted_iota(jnp.int32, (q, q), 0)
    ci = lax.broadcasted_iota(jnp.int32, (q, q), 1)
    causal = ri >= ci
    acum_row = _mm(dta, (ri <= ci).astype(F32), HIGHEST)
    acum_col = _nt(causal.astype(F32), dta, HIGHEST)
    to_end = jnp.exp(acum_row[:, q - 1:q] - acum_row) * dt
    exp_acol = jnp.exp(acum_col)
    dec_last = exp_acol[q - 1:q, :]
    arow_dt = acum_row - jnp.log(dt)

    b0 = d_inner
    c0_ = d_inner + n_groups * n_state
    for g in range(n_groups):
        bg = xc[:, b0 + g * n_state:b0 + (g + 1) * n_state]
        cg = xc[:, c0_ + g * n_state:c0_ + (g + 1) * n_state]
        cb = _nt(cg, bg)
        st = state[g]
        y_int = _nt(cg, st.astype(BF16))
        ys, te, dec = [], [], []
        for r in range(r_heads):
            h = g * r_heads + r
            seg = acum_col[:, h:h + 1] - arow_dt[h:h + 1, :]
            w = (cb * jnp.exp(jnp.where(causal, seg, -jnp.inf))).astype(BF16)
            y_h = _mm(w, xc[:, h * p:(h + 1) * p])
            ys.append(y_h + y_int[:, r * p:(r + 1) * p] * exp_acol[:, h:h + 1])
            te.append(jnp.broadcast_to(to_end[h:h + 1, :], (p, q)))
            dec.append(jnp.broadcast_to(dec_last[:, h:h + 1], (p, n_state)))
        xg = xs[:, g * gp:(g + 1) * gp]
        y_s[:, g * gp:(g + 1) * gp] = (jnp.concatenate(ys, axis=1)
                                       + dsk_ref[:, g * gp:(g + 1) * gp] * xg)
        xw = (xg.T * jnp.concatenate(te, axis=0)).astype(BF16)
        state[g] = st * jnp.concatenate(dec, axis=0) + _mm(xw, bg)

    yg = y_s[...] * _silu(z_s[...])
    yn = _rms(yg, nw_ref[...]).astype(BF16)
    o_ref[...] = xres_ref[...] + m[:, 2 * d:] * _mm(yn, wo_ref[...])


def _ssd(x2, m, g, w_zx, w_dt_t, cw, cb, dtb, alog, dsk, nw, wo, bsz, seq):
    t, d = x2.shape
    d_inner = wo.shape[0]
    conv_dim = w_zx.shape[1] - d_inner
    q = SSM_CHUNK
    nc = seq // q
    gp = d_inner // SSM_GROUPS
    row = lambda b, c: (b * nc + c, 0)
    return pl.pallas_call(
        _ssd_kernel,
        grid=(bsz, nc),
        in_specs=[pl.BlockSpec((q, d), row),
                  pl.BlockSpec((None, 1, 3 * d), lambda b, c: (b, 0, 0)),
                  _resident(g.shape), _resident(w_zx.shape),
                  _resident(w_dt_t.shape),
                  _resident(cw.shape), _resident(cb.shape),
                  _resident(dtb.shape), _resident(alog.shape),
                  _resident(dsk.shape), _resident(nw.shape),
                  _resident(wo.shape)],
        out_specs=pl.BlockSpec((q, d), row),
        out_shape=jax.ShapeDtypeStruct((t, d), F32),
        scratch_shapes=[pltpu.VMEM((q + 2 * SUBLANES, conv_dim), F32),
                        pltpu.VMEM((q, conv_dim), BF16),
                        pltpu.VMEM((q, d_inner), F32),
                        pltpu.VMEM((q, d_inner), F32),
                        pltpu.VMEM((q, d_inner), F32),
                        pltpu.VMEM((SSM_GROUPS, gp, SSM_STATE), F32)],
        compiler_params=_params(2),
        name="mamba2_mixer",
    )(x2, m, g, w_zx, w_dt_t, cw, cb, dtb, alog, dsk, nw, wo)


KPOS_LANE = 96
SLOT = LANES


def _qkv_kernel(x_ref, m_ref, g_ref, wqt_ref, wk_ref, wvt_ref,
                qt_ref, k_ref, vt_ref, km_ref, *, per_b):
    tm, d = x_ref.shape
    blk = MOBA_BLOCK
    dh = ATTN_HEAD_DIM
    nbt = tm // blk
    scale = dh ** -0.5 * LOG2_E
    h = _modulated(x_ref[...], g_ref[...], m_ref[...], d).astype(BF16)

    tr = 256
    for r0 in range(0, d, tr):
        qt_ref[r0:r0 + tr, :] = (_nt(wqt_ref[r0:r0 + tr, :], h) * scale).astype(BF16)

    tn = 512
    row = lax.broadcasted_iota(jnp.int32, (tm, tn), 0)
    lane = lax.broadcasted_iota(jnp.int32, (tm, tn), 1) % SLOT
    blk_id = (pl.program_id(0) % per_b) * nbt + row // blk
    pos = (row % blk).astype(F32)
    extras = jnp.where(lane == dh + blk_id, 1.0,
                       jnp.where((lane >= KPOS_LANE) & (lane < KPOS_LANE + 3),
                                 pos, 0.0))
    for n0 in range(0, k_ref.shape[1], tn):
        kk = _mm(h, wk_ref[:, n0:n0 + tn])
        k_ref[:, n0:n0 + tn] = (kk + extras).astype(BF16)
        for j in range(nbt):
            km_ref[j, :, n0:n0 + tn] = jnp.mean(kk[j * blk:(j + 1) * blk, :],
                                                axis=0, keepdims=True)

    tail = jnp.where(lax.broadcasted_iota(jnp.int32, (SLOT - dh, blk), 0) == 0,
                     1.0, 0.0).astype(BF16)
    for r0 in range(0, d, tr):
        vt = _nt(wvt_ref[r0:r0 + tr, :], h).astype(BF16)
        for hh in range(tr // dh):
            s0 = (r0 // dh + hh) * SLOT
            for j in range(nbt):
                vt_ref[j, s0:s0 + dh, :] = vt[hh * dh:(hh + 1) * dh,
                                              j * blk:(j + 1) * blk]
                vt_ref[j, s0 + dh:s0 + SLOT, :] = tail


def _qkv(x2, m, g, wq_t, wk_pad, wv_t_pad, bsz, seq):
    t, d = x2.shape
    dp = wk_pad.shape[1]
    blk = MOBA_BLOCK
    tm = 512
    per_b = seq // tm
    nbt = tm // blk
    return pl.pallas_call(
        functools.partial(_qkv_kernel, per_b=per_b),
        grid=(t // tm,),
        in_specs=[pl.BlockSpec((tm, d), lambda i: (i, 0)),
                  pl.BlockSpec((None, 1, 3 * d), lambda i: (i // per_b, 0, 0)),
                  _resident((1, d)), _resident(wq_t.shape),
                  _resident(wk_pad.shape), _resident(wv_t_pad.shape)],
        out_specs=[pl.BlockSpec((None, d, tm), lambda i: (i // per_b, 0, i % per_b)),
                   pl.BlockSpec((tm, dp), lambda i: (i, 0)),
                   pl.BlockSpec((None, nbt, dp, blk),
                                lambda i: (i // per_b, i % per_b, 0, 0)),
                   pl.BlockSpec((nbt, 1, dp), lambda i: (i, 0, 0))],
        out_shape=[jax.ShapeDtypeStruct((bsz, d, seq), BF16),
                   jax.ShapeDtypeStruct((t, dp), BF16),
                   jax.ShapeDtypeStruct((bsz, seq // blk, dp, blk), BF16),
                   jax.ShapeDtypeStruct((t // blk, 1, dp), F32)],
        compiler_params=_params(1),
        name="moba_qkv_proj",
    )(x2, m, g, wq_t, wk_pad, wv_t_pad)


MOBA_HEADS_PER_STEP = 8


def _moba_kernel(sl_ref, qt_ref, k_ref, vt_ref, km_ref, aug_ref, o_ref, s_buf):
    hpg = aug_ref.shape[0]
    i = pl.program_id(2)
    blk = qt_ref.shape[1]
    dh = qt_ref.shape[0] // hpg
    nb = km_ref.shape[0]
    heads = range(hpg)
    step = [sl_ref[pl.program_id(1) * hpg + hh] * blk for hh in heads]

    bid = lax.broadcasted_iota(jnp.int32, (nb, blk), 0)
    qa, qa_own = [], []
    for hh in heads:
        qt = qt_ref[hh * dh:(hh + 1) * dh, :]
        km = km_ref[:, hh * SLOT:hh * SLOT + dh]
        hi = km.astype(BF16)
        lo = (km - hi.astype(F32)).astype(BF16)
        gate = _mm(hi, qt) + _mm(lo, qt)
        gate = jnp.where(bid < i, gate, -jnp.inf)
        bias = jnp.where(bid == i, 0.0, NEG_BIG)
        for _ in range(MOBA_TOPK):
            mx = jnp.max(gate, axis=0, keepdims=True)
            first = jnp.min(jnp.where(gate == mx, bid, nb), axis=0, keepdims=True)
            pick = (bid == first) & (mx > -jnp.inf)
            bias = jnp.where(pick, 0.0, bias)
            gate = jnp.where(pick, -jnp.inf, gate)
        qa.append(jnp.concatenate([qt, bias.astype(BF16), aug_ref[hh]], axis=0))
        qa_own.append(jnp.concatenate([qt, jnp.zeros((nb, blk), BF16), aug_ref[hh]],
                                      axis=0))

    kid = lax.broadcasted_iota(jnp.int32, (blk, blk), 0)
    qid = lax.broadcasted_iota(jnp.int32, (blk, blk), 1)
    causal = kid <= qid
    own0 = pl.multiple_of(i * blk, blk)

    def k_rows(hh, r0, rows):
        return k_ref[pl.ds(r0, rows), hh * SLOT:(hh + 1) * SLOT]

    def v_blk(hh, n):
        return vt_ref[n, hh * SLOT:(hh + 1) * SLOT, :]

    n_pairs = (i + 1) // 2
    last_pair = jnp.maximum(n_pairs - 1, 0)

    def stage_scores(buf, t):
        r0 = pl.multiple_of(jnp.minimum(t, last_pair) * (2 * blk), 2 * blk)
        tops = []
        for hh in heads:
            s = _mm(k_rows(hh, r0, 2 * blk), qa[hh])
            s_buf[buf, hh] = s
            tops.append((jnp.max(s[:blk], axis=0, keepdims=True),
                         jnp.max(s[blk:], axis=0, keepdims=True)))
        return tuple(tops)

    def absorb(buf, t, tops, state):
        n0 = 2 * t
        n1 = n0 + 1
        out = []
        for hh in heads:
            m, acc = state[hh]
            c0 = jnp.where(n0 < i, step[hh] * (n0 - i).astype(F32), NEG_BIG)
            c1 = jnp.where(n1 < i, step[hh] * (n1 - i).astype(F32), NEG_BIG)
            m_new = jnp.maximum(m, jnp.maximum(tops[hh][0] + c0, tops[hh][1] + c1))
            p0 = jnp.exp2(s_buf[buf, hh, :blk] - (m_new - c0)).astype(BF16)
            p1 = jnp.exp2(s_buf[buf, hh, blk:] - (m_new - c1)).astype(BF16)
            acc = (acc * jnp.exp2(m - m_new)
                   + _mm(v_blk(hh, jnp.minimum(n0, nb - 1)), p0)
                   + _mm(v_blk(hh, jnp.minimum(n1, nb - 1)), p1))
            out.append((m_new, acc))
        return tuple(out)

    s_own = [_mm(k_rows(hh, own0, blk), qa_own[hh]) for hh in heads]
    tops_a = stage_scores(0, 0)
    state = []
    for hh in heads:
        s = jnp.where(causal, s_own[hh], NEG_BIG)
        m = jnp.max(s, axis=0, keepdims=True)
        state.append((m, _mm(v_blk(hh, i), jnp.exp2(s - m).astype(BF16))))

    def visit_two_pairs(t2, carry):
        state, tops_a = carry
        t = 2 * t2
        tops_b = stage_scores(1, t + 1)
        state = absorb(0, t, tops_a, state)
        tops_a = stage_scores(0, t + 2)
        state = absorb(1, t + 1, tops_b, state)
        return state, tops_a

    fin, _ = lax.fori_loop(0, (n_pairs + 1) // 2, visit_two_pairs,
                           (tuple(state), tops_a))
    for hh in heads:
        acc = fin[hh][1]
        o_ref[hh * dh:(hh + 1) * dh, :] = (acc[:dh] / acc[dh:dh + 1]).astype(o_ref.dtype)


def _moba_attention(qt, kpad, vt, kmean, aug, slopes, bsz, seq):
    d = qt.shape[1]
    dh = ATTN_HEAD_DIM
    blk = MOBA_BLOCK
    nb = seq // blk
    hpg = MOBA_HEADS_PER_STEP
    n_groups = d // dh // hpg
    dp = kpad.shape[1]
    qspec = pl.BlockSpec((None, hpg * dh, blk), lambda b, h, i: (b, h, i))
    return pl.pallas_call(
        _moba_kernel,
        grid=(bsz, n_groups, nb),
        in_specs=[pl.BlockSpec(memory_space=pltpu.SMEM),
                  qspec,
                  pl.BlockSpec((None, seq, hpg * SLOT), lambda b, h, i: (b, 0, h),
                               pipeline_mode=pl.Buffered(1)),
                  pl.BlockSpec((None, nb, hpg * SLOT, blk), lambda b, h, i: (b, 0, h, 0),
                               pipeline_mode=pl.Buffered(1)),
                  pl.BlockSpec((None, nb, hpg * SLOT), lambda b, h, i: (b, 0, h)),
                  pl.BlockSpec((hpg, SLOT - dh - nb, blk), lambda b, h, i: (h, 0, 0))],
        out_specs=qspec,
        out_shape=jax.ShapeDtypeStruct((bsz, d, seq), BF16),
        scratch_shapes=[pltpu.VMEM((2, hpg, 2 * blk, blk), F32)],
        compiler_params=_params(3),
        name="moba_attention",
    )(slopes, qt, kpad.reshape(bsz, seq, dp), vt, kmean, aug)


def _proj_res_kernel(at_ref, x_ref, m_ref, w_ref, o_ref):
    d = x_ref.shape[1]
    y = lax.dot_general(at_ref[...], w_ref[...], (((0,), (0,)), ((), ())),
                        preferred_element_type=F32)
    o_ref[...] = x_ref[...] + m_ref[:, 2 * d:] * y


def _proj_res(at, x2, m, w, seq):
    t, d = x2.shape
    tm = 512
    per_b = seq // tm
    return pl.pallas_call(
        _proj_res_kernel,
        grid=(t // tm,),
        in_specs=[pl.BlockSpec((None, at.shape[1], tm),
                               lambda i: (i // per_b, 0, i % per_b)),
                  pl.BlockSpec((tm, d), lambda i: (i, 0)),
                  pl.BlockSpec((None, 1, 3 * d), lambda i: (i // per_b, 0, 0)),
                  _resident(w.shape)],
        out_specs=pl.BlockSpec((tm, d), lambda i: (i, 0)),
        out_shape=jax.ShapeDtypeStruct((t, d), F32),
        compiler_params=_params(1),
        name="attn_out_proj",
    )(at, x2, m, w)


def _moba_constants(d, seq):
    n_heads = d // ATTN_HEAD_DIM
    nb = seq // MOBA_BLOCK
    slopes = jnp.exp2(-ALIBI_MAX_BIAS * (jnp.arange(n_heads, dtype=F32) + 1.0) / n_heads)
    slopes = slopes * LOG2_E
    hi = slopes.astype(BF16)
    lo = (slopes - hi.astype(F32)).astype(BF16)
    lo2 = (slopes - hi.astype(F32) - lo.astype(F32)).astype(BF16)
    rows = SLOT - ATTN_HEAD_DIM - nb
    aug = jnp.zeros((n_heads, rows), BF16)
    first = KPOS_LANE - ATTN_HEAD_DIM - nb
    aug = aug.at[:, first].set(hi).at[:, first + 1].set(lo).at[:, first + 2].set(lo2)
    return slopes, jnp.broadcast_to(aug[:, :, None], (n_heads, rows, MOBA_BLOCK))


def _pad_heads(w, axis):
    shape = list(w.shape)
    n_heads = shape[axis] // ATTN_HEAD_DIM
    w = w.reshape(shape[:axis] + [n_heads, ATTN_HEAD_DIM] + shape[axis + 1:])
    pads = [(0, 0)] * w.ndim
    pads[axis + 1] = (0, SLOT - ATTN_HEAD_DIM)
    w = jnp.pad(w, pads)
    return w.reshape(shape[:axis] + [n_heads * SLOT] + shape[axis + 1:])


def kernel(x, c, norm_g, mod_w, mod_b, ffn_w_gate, ffn_w_up, ffn_w_down,
           ssm_w_in, ssm_conv_w, ssm_conv_b, ssm_dt_bias, ssm_a_log, ssm_d,
           ssm_norm_w, ssm_w_out, attn_w_qkv, attn_w_out, final_norm_g):
    bsz, seq, d = x.shape
    depth = norm_g.shape[0]
    d_inner = ssm_w_out.shape[1]
    heads = ssm_dt_bias.shape[1]
    assert seq // MOBA_BLOCK == SLOT - ATTN_HEAD_DIM - (SLOT - KPOS_LANE)

    m_all = _mod_params(c, mod_w, mod_b)
    x2 = x.reshape(bsz * seq, d)
    slopes, aug = _moba_constants(d, seq)

    for i in range(depth):
        ms = [m_all[3 * i + j].reshape(bsz, 1, 3 * d) for j in range(3)]
        gs = [norm_g[i, j].reshape(1, d) for j in range(3)]
        x2 = _ffn(x2, ms[0], gs[0], ffn_w_gate[i, 0].astype(BF16),
                  ffn_w_up[i, 0].astype(BF16), ffn_w_down[i, 0].astype(BF16), seq)
        j = i // 2
        if i % 2 == 0:
            n_zx = ssm_w_in.shape[2] - heads
            x2 = _ssd(x2, ms[1], gs[1],
                      ssm_w_in[j][:, :n_zx].astype(BF16),
                      ssm_w_in[j][:, n_zx:].T.astype(BF16),
                      ssm_conv_w[j].reshape(SSM_CONV, -1),
                      ssm_conv_b[j].reshape(1, -1),
                      ssm_dt_bias[j].reshape(heads, 1),
                      ssm_a_log[j].reshape(heads, 1),
                      jnp.repeat(ssm_d[j], SSM_HEAD_DIM).reshape(1, d_inner),
                      ssm_norm_w[j].reshape(1, d_inner),
                      ssm_w_out[j].astype(BF16), bsz, seq)
        else:
            wqkv = attn_w_qkv[j].astype(BF16)
            qt, kpad, vt, km = _qkv(x2, ms[1], gs[1], wqkv[:, :d].T,
                                    _pad_heads(wqkv[:, d:2 * d], 1),
                                    wqkv[:, 2 * d:].T, bsz, seq)
            att_t = _moba_attention(qt, kpad, vt,
                                    km.reshape(bsz, seq // MOBA_BLOCK, -1),
                                    aug, slopes, bsz, seq)
            x2 = _proj_res(att_t, x2, ms[1], attn_w_out[j].astype(BF16), seq)
        last = i == depth - 1
        x2 = _ffn(x2, ms[2], gs[2], ffn_w_gate[i, 1].astype(BF16),
                  ffn_w_up[i, 1].astype(BF16), ffn_w_down[i, 1].astype(BF16), seq,
                  final_g=final_norm_g.reshape(1, d) if last else None)
    return x2.reshape(bsz, seq, d)
```
